```python
import math
import jax, jax.numpy as jnp
from jax import lax
import numpy as np

D_MODEL = 1024
BATCH = 16
SEQ = 2048
DEPTH = 1

CONV_CH = 512
CONV_KERNEL = 31
ATTN_HEADS = 4
ATTN_HEAD_DIM = 64
ATTN_V_DIM = 2 * ATTN_HEAD_DIM
ATTN_WIDTH = ATTN_HEADS * ATTN_V_DIM
MIX_WIDTH = CONV_CH + ATTN_WIDTH
IN_COLS = 2 * CONV_CH + 3 * ATTN_WIDTH
N_EXPERTS = 32
TOP_K = 4
D_FF = D_MODEL
SWIGLU_LIMIT = 7.0
SWIGLU_ALPHA = 1.702
Q_BLOCK = 128
ROW_BLOCK = 256
N_MOD = 6
EPS = 1e-5

kernel_name = "hymba_conformer_diffattn_moe_adaln"


def rms_norm(x, g):
    xf = x.astype(jnp.float32)
    y = xf * lax.rsqrt(jnp.mean(xf * xf, axis=-1, keepdims=True) + EPS)
    return (y * g.astype(jnp.float32)).astype(x.dtype)


def layer_norm(x, g, b):
    xf = x.astype(jnp.float32)
    mu = jnp.mean(xf, axis=-1, keepdims=True)
    xc = xf - mu
    y = xc * lax.rsqrt(jnp.mean(xc * xc, axis=-1, keepdims=True) + EPS)
    return (y * g.astype(jnp.float32) + b.astype(jnp.float32)).astype(x.dtype)


def alibi_slopes(n_heads):
    return jnp.exp2(-8.0 * jnp.arange(1, n_heads + 1, dtype=jnp.float32) / n_heads)


def conformer_conv(u, conv_w, conv_b, ln_g, ln_b):
    a, g = jnp.split(u, 2, axis=-1)
    h = a * jax.nn.sigmoid(g)
    h = lax.conv_general_dilated(
        h, conv_w[:, None, :].astype(h.dtype), window_strides=(1,),
        padding=[(CONV_KERNEL // 2, CONV_KERNEL // 2)],
        dimension_numbers=("NWC", "WIO", "NWC"),
        feature_group_count=CONV_CH) + conv_b
    h = layer_norm(h, ln_g, ln_b)
    return jax.nn.silu(h)


def diff_attention(q, k, v, lam, subln_g, lam_init):
    B, S = q.shape[0], q.shape[1]
    n_blocks = S // Q_BLOCK
    scale = ATTN_HEAD_DIM ** -0.5
    slopes = alibi_slopes(ATTN_HEADS)[None, :, None, None, None]
    kpos = jnp.arange(S, dtype=jnp.float32)
    qb = q.reshape(B, n_blocks, Q_BLOCK, ATTN_HEADS, 2, ATTN_HEAD_DIM).transpose(1, 0, 2, 3, 4, 5)

    def block(args):
        q_blk, j = args
        s = jnp.einsum("bqhcd,bkhcd->bhcqk", q_blk, k).astype(jnp.float32) * scale
        qpos = (j * Q_BLOCK + jnp.arange(Q_BLOCK)).astype(jnp.float32)
        dist = jnp.abs(qpos[:, None] - kpos[None, :])
        p = jax.nn.softmax(s - slopes * dist, axis=-1)
        p = p[:, :, 0] - lam * p[:, :, 1]
        return jnp.einsum("bhqk,bkhe->bqhe", p.astype(v.dtype), v)

    o = lax.map(block, (qb, jnp.arange(n_blocks)))
    o = o.transpose(1, 0, 2, 3, 4).reshape(B, S, ATTN_HEADS, ATTN_V_DIM)
    o = rms_norm(o, subln_g) * (1.0 - lam_init)
    return o.reshape(B, S, ATTN_WIDTH)


def routed_experts(h, w_router, b_router, w_gate_up, b_gate_up, w_down, b_down):
    B, S, D = h.shape
    N = B * S
    t = h.reshape(N, D)
    logits = (t @ w_router + b_router).astype(jnp.float32)
    top_v, top_i = lax.top_k(logits, TOP_K)
    gates = jax.nn.softmax(top_v, axis=-1)
    A = N * TOP_K
    e_flat = top_i.reshape(A)
    order = jnp.argsort(e_flat)
    e_sorted = e_flat[order]
    tok_sorted = order // TOP_K
    gate_sorted = gates.reshape(A)[order]
    counts = jnp.bincount(e_flat, length=N_EXPERTS)
    padded = (counts + ROW_BLOCK - 1) // ROW_BLOCK * ROW_BLOCK
    start = jnp.cumsum(counts) - counts
    pend = jnp.cumsum(padded)
    pstart = pend - padded
    dest = pstart[e_sorted] + jnp.arange(A) - start[e_sorted]
    n_blocks = -(-A // ROW_BLOCK) + N_EXPERTS
    P = n_blocks * ROW_BLOCK
    buf_tok = jnp.zeros((P,), jnp.int32).at[dest].set(tok_sorted.astype(jnp.int32))
    buf_gate = jnp.zeros((P,), jnp.float32).at[dest].set(gate_sorted)
    blk_expert = jnp.minimum(
        jnp.searchsorted(pend, jnp.arange(n_blocks) * ROW_BLOCK, side="right"), N_EXPERTS - 1)

    def expert_block(args):
        tok, e = args
        xb = t[tok]
        gu = xb @ w_gate_up[e] + b_gate_up[e]
        g, u = jnp.split(gu, 2, axis=-1)
        g = jnp.minimum(g, SWIGLU_LIMIT)
        u = jnp.clip(u, -SWIGLU_LIMIT, SWIGLU_LIMIT)
        y = (u + 1.0) * (g * jax.nn.sigmoid(SWIGLU_ALPHA * g))
        return y @ w_down[e] + b_down[e]

    out = lax.map(expert_block, (buf_tok.reshape(n_blocks, ROW_BLOCK), blk_expert))
    out = out.reshape(P, D) * buf_gate[:, None].astype(out.dtype)
    y = jax.ops.segment_sum(out, buf_tok, num_segments=N)
    return y.reshape(B, S, D)


def setup_inputs(seed: int = 0) -> dict:
    key = jax.random.key(seed)
    ks = jax.random.split(key, 24)
    L, D, E = DEPTH, D_MODEL, N_EXPERTS
    n = lambda k, shape, s: jax.random.normal(k, shape, jnp.float32) * s
    return {
        "x": n(ks[0], (BATCH, SEQ, D), 1.0),
        "c": n(ks[1], (BATCH, D), 1.0),
        "w_ada": n(ks[2], (L, D, N_MOD * D), 0.5 * D ** -0.5),
        "b_ada": n(ks[3], (L, N_MOD * D), 0.02),
        "norm1_g": 1.0 + n(ks[4], (L, D), 0.02),
        "w_in": n(ks[5], (L, D, IN_COLS), D ** -0.5),
        "q_norm_g": 1.0 + n(ks[6], (L, ATTN_HEAD_DIM), 0.02),
        "k_norm_g": 1.0 + n(ks[7], (L, ATTN_HEAD_DIM), 0.02),
        "lambda_q1": n(ks[8], (L, ATTN_HEAD_DIM), 0.1),
        "lambda_k1": n(ks[9], (L, ATTN_HEAD_DIM), 0.1),
        "lambda_q2": n(ks[10], (L, ATTN_HEAD_DIM), 0.1),
        "lambda_k2": n(ks[11], (L, ATTN_HEAD_DIM), 0.1),
        "subln_g": 1.0 + n(ks[12], (L, ATTN_V_DIM), 0.02),
        "conv_w": n(ks[13], (L, CONV_KERNEL, CONV_CH), CONV_KERNEL ** -0.5),
        "conv_b": n(ks[14], (L, CONV_CH), 0.02),
        "conv_ln_g": 1.0 + n(ks[15], (L, CONV_CH), 0.02),
        "conv_ln_b": n(ks[16], (L, CONV_CH), 0.02),
        "w_out": n(ks[17], (L, MIX_WIDTH, D), MIX_WIDTH ** -0.5),
        "norm2_g": 1.0 + n(ks[18], (L, D), 0.02),
        "w_router": n(ks[19], (L, D, E), D ** -0.5),
        "b_router": n(ks[20], (L, E), 0.01),
        "w_gate_up": n(ks[21], (L, E, D, 2 * D_FF), D ** -0.5),
        "b_gate_up": n(ks[22], (L, E, 2 * D_FF), 0.02),
        "w_down": n(ks[23], (L, E, D_FF, D), D_FF ** -0.5),
        "b_down": n(jax.random.fold_in(key, 99), (L, E, D), 0.02),
    }


def reference(x, c, w_ada, b_ada, norm1_g, w_in, q_norm_g, k_norm_g, lambda_q1, lambda_k1,
              lambda_q2, lambda_k2, subln_g, conv_w, conv_b, conv_ln_g, conv_ln_b, w_out,
              norm2_g, w_router, b_router, w_gate_up, b_gate_up, w_down, b_down):
    B, S, D = x.shape
    for l in range(DEPTH):
        mod = jax.nn.silu(c) @ w_ada[l] + b_ada[l]
        shift1, scale1, gate1, shift2, scale2, gate2 = [m[:, None, :] for m in jnp.split(mod, N_MOD, axis=-1)]

        h = rms_norm(x, norm1_g[l]) * (1.0 + scale1) + shift1
        proj = h @ w_in[l]
        u_conv = proj[..., :2 * CONV_CH]
        q, k, v = jnp.split(proj[..., 2 * CONV_CH:], 3, axis=-1)
        q = rms_norm(q.reshape(B, S, ATTN_HEADS, 2, ATTN_HEAD_DIM), q_norm_g[l])
        k = rms_norm(k.reshape(B, S, ATTN_HEADS, 2, ATTN_HEAD_DIM), k_norm_g[l])
        v = v.reshape(B, S, ATTN_HEADS, ATTN_V_DIM)
        lam_init = 0.8 - 0.6 * math.exp(-0.3 * l)
        lam = (jnp.exp(jnp.sum(lambda_q1[l] * lambda_k1[l]).astype(jnp.float32))
               - jnp.exp(jnp.sum(lambda_q2[l] * lambda_k2[l]).astype(jnp.float32)) + lam_init)
        attn_out = diff_attention(q, k, v, lam, subln_g[l], lam_init)
        conv_out = conformer_conv(u_conv, conv_w[l], conv_b[l], conv_ln_g[l], conv_ln_b[l])
        mix = jnp.concatenate([conv_out, attn_out], axis=-1) @ w_out[l]
        x = x + gate1 * mix

        h2 = rms_norm(x, norm2_g[l]) * (1.0 + scale2) + shift2
        y = routed_experts(h2, w_router[l], b_router[l], w_gate_up[l], b_gate_up[l], w_down[l], b_down[l])
        x = x + gate2 * y
    return x
```

```python
import functools
import math

import jax
import jax.numpy as jnp
from jax import lax
from jax.experimental import pallas as pl
from jax.experimental.pallas import tpu as pltpu

F32 = jnp.float32
BF16 = jnp.bfloat16
I32 = jnp.int32
HIGHEST = lax.Precision.HIGHEST

CONV_CH = 512
CONV_KERNEL = 31
CONV_HALO = 16
ATTN_HEADS = 4
ATTN_HEAD_DIM = 64
ATTN_V_DIM = 2 * ATTN_HEAD_DIM
ATTN_WIDTH = ATTN_HEADS * ATTN_V_DIM
N_EXPERTS = 32
TOP_K = 4
SWIGLU_LIMIT = 7.0
SWIGLU_ALPHA = 1.702
N_MOD = 6
EPS = 1e-5

V7X_LANES = 128
V7X_SUBLANES = 8

INPROJ_ROWS = 512
ATTN_Q_ROWS = 256
TOK_TILE = 256
CHUNK = V7X_SUBLANES
BLOCK_ROWS = 256
CHUNKS_PER_BLOCK = BLOCK_ROWS // CHUNK


def _round_up(a, b):
    return (a + b - 1) // b * b


SORT_ROWS = _round_up(TOP_K * TOK_TILE + N_EXPERTS * (CHUNK - 1) + CHUNK, V7X_LANES)
SORT_CHUNKS = SORT_ROWS // CHUNK
USED_CHUNKS_MAX = (TOP_K * TOK_TILE + N_EXPERTS * (CHUNK - 1)) // CHUNK


def _sigmoid(v):
    return 1.0 / (1.0 + jnp.exp(-v))


def _nt_dims():
    return (((1,), (1,)), ((), ()))


def _ada_kernel(c_ref, w_ref, b_ref, lq1_ref, lk1_ref, lq2_ref, lk2_ref, mod_ref, lam_ref, *, lam_init):
    c = c_ref[...]
    sc = c * _sigmoid(c)
    mod_ref[...] = jnp.dot(sc, w_ref[...], precision=HIGHEST, preferred_element_type=F32) + b_ref[...]
    s1 = jnp.sum(lq1_ref[...] * lk1_ref[...], axis=-1, keepdims=True)
    s2 = jnp.sum(lq2_ref[...] * lk2_ref[...], axis=-1, keepdims=True)
    lam = jnp.exp(s1) - jnp.exp(s2) + lam_init
    lam_ref[...] = jnp.broadcast_to(lam, lam_ref.shape)


def _ada(c, w_ada, b_ada, lq1, lk1, lq2, lk2, lam_init):
    B, D = c.shape
    cols = w_ada.shape[1]
    bc = D
    vec = pl.BlockSpec((1, ATTN_HEAD_DIM), lambda j: (0, 0))
    return pl.pallas_call(
        functools.partial(_ada_kernel, lam_init=lam_init),
        name="ada",
        grid=(cols // bc,),
        in_specs=[
            pl.BlockSpec((B, D), lambda j: (0, 0)),
            pl.BlockSpec((D, bc), lambda j: (0, j)),
            pl.BlockSpec((1, bc), lambda j: (0, j)),
            vec, vec, vec, vec,
        ],
        out_specs=[
            pl.BlockSpec((B, bc), lambda j: (0, j)),
            pl.BlockSpec((V7X_SUBLANES, V7X_LANES), lambda j: (0, 0)),
        ],
        out_shape=[
            jax.ShapeDtypeStruct((B, cols), F32),
            jax.ShapeDtypeStruct((V7X_SUBLANES, V7X_LANES), F32),
        ],
        compiler_params=pltpu.CompilerParams(dimension_semantics=("arbitrary",)),
    )(c, w_ada, b_ada, lq1, lk1, lq2, lk2)


def _group_rms_scale(t, lo):
    sq = t * t
    s1 = jnp.sum(jnp.where(lo, sq, 0.0), axis=-1, keepdims=True)
    s2 = jnp.sum(jnp.where(lo, 0.0, sq), axis=-1, keepdims=True)
    r1 = lax.rsqrt(s1 * (1.0 / ATTN_HEAD_DIM) + EPS)
    r2 = lax.rsqrt(s2 * (1.0 / ATTN_HEAD_DIM) + EPS)
    return jnp.where(lo, r1, r2)


def _inproj_kernel(x_ref, mod_ref, g1_ref, w_ref, qg_ref, kg_ref, hglu_ref, q_ref, k_ref, v_ref):
    x = x_ref[...]
    ms = jnp.mean(x * x, axis=-1, keepdims=True)
    shift = mod_ref[0:1, :]
    scale = mod_ref[1:2, :]
    h = ((x * lax.rsqrt(ms + EPS)) * g1_ref[...]) * (1.0 + scale) + shift
    hb = h.astype(BF16)
    c2 = 2 * CONV_CH
    ag = jnp.dot(hb, w_ref[:, 0:c2], preferred_element_type=F32)
    hglu_ref[...] = ag[:, :CONV_CH] * _sigmoid(ag[:, CONV_CH:])
    lo = lax.broadcasted_iota(I32, (1, V7X_LANES), 1) < ATTN_HEAD_DIM
    for src_col, g_ref, o_ref in ((c2, qg_ref, q_ref), (c2 + ATTN_WIDTH, kg_ref, k_ref)):
        t = jnp.dot(hb, w_ref[:, src_col:src_col + ATTN_WIDTH], preferred_element_type=F32)
        for hd in range(ATTN_HEADS):
            sl = slice(hd * V7X_LANES, (hd + 1) * V7X_LANES)
            th = t[:, sl]
            o_ref[:, sl] = (th * _group_rms_scale(th, lo) * g_ref[:, sl]).astype(BF16)
    v0 = c2 + 2 * ATTN_WIDTH
    v_ref[...] = jnp.dot(hb, w_ref[:, v0:v0 + ATTN_WIDTH], preferred_element_type=F32).astype(BF16)


def _inproj(x2d, mod3, g1, w_in_b, qg, kg, seq):
    N, D = x2d.shape
    tm = INPROJ_ROWS
    steps_per_seq = seq // tm
    cols = w_in_b.shape[1]
    row = lambda i: (i, 0)
    const = lambda i: (0, 0)
    return pl.pallas_call(
        _inproj_kernel,
        name="inproj",
        grid=(N // tm,),
        in_specs=[
            pl.BlockSpec((tm, D), row),
            pl.BlockSpec((None, N_MOD, D), lambda i: (i // steps_per_seq, 0, 0)),
            pl.BlockSpec((1, D), const),
            pl.BlockSpec((D, cols), const),
            pl.BlockSpec((1, ATTN_WIDTH), const),
            pl.BlockSpec((1, ATTN_WIDTH), const),
        ],
        out_specs=[
            pl.BlockSpec((tm, CONV_CH), row),
            pl.BlockSpec((tm, ATTN_WIDTH), row),
            pl.BlockSpec((tm, ATTN_WIDTH), row),
            pl.BlockSpec((tm, ATTN_WIDTH), row),
        ],
        out_shape=[
            jax.ShapeDtypeStruct((N, CONV_CH), F32),
            jax.ShapeDtypeStruct((N, ATTN_WIDTH), BF16),
            jax.ShapeDtypeStruct((N, ATTN_WIDTH), BF16),
            jax.ShapeDtypeStruct((N, ATTN_WIDTH), BF16),
        ],
        compiler_params=pltpu.CompilerParams(
            dimension_semantics=("parallel",), vmem_limit_bytes=48 * 1024 * 1024),
    )(x2d, mod3, g1, w_in_b, qg, kg)


def _attn_kernel(lam_ref, slope_ref, q_ref, k_ref, v_ref, sg_ref, o_ref, *, tq, lam_init):
    hd = pl.program_id(1)
    i = pl.program_id(2)
    lam = lam_ref[0]
    slope = slope_ref[hd]
    q = q_ref[...]
    k = k_ref[...]
    v = v_ref[...]
    S = k.shape[0]
    lo = lax.broadcasted_iota(I32, (1, V7X_LANES), 1) < ATTN_HEAD_DIM
    zero = jnp.zeros_like(q)
    qpos = i * tq + lax.broadcasted_iota(I32, (tq, S), 0)
    kpos = lax.broadcasted_iota(I32, (tq, S), 1)
    bias = jnp.abs(qpos - kpos).astype(F32) * slope

    def softmax_parts(qc):
        s = lax.dot_general(qc, k, _nt_dims(), preferred_element_type=F32) - bias
        m = jnp.max(s, axis=-1, keepdims=True)
        p = jnp.exp(s - m)
        return p, jnp.sum(p, axis=-1, keepdims=True)

    p1, l1 = softmax_parts(jnp.where(lo, q, zero))
    p2, l2 = softmax_parts(jnp.where(lo, zero, q))
    p = p1 * (1.0 / l1) - p2 * (lam / l2)
    o = jnp.dot(p.astype(BF16), v, preferred_element_type=F32)
    ms = jnp.mean(o * o, axis=-1, keepdims=True)
    o = (o * lax.rsqrt(ms + EPS)) * sg_ref[...] * (1.0 - lam_init)
    o_ref[...] = o.astype(BF16)


def _attention(lam1, slopes, q, k, v, sg, batch, seq, lam_init):
    tq = ATTN_Q_ROWS
    nq = seq // tq
    smem = pl.BlockSpec(memory_space=pltpu.SMEM)
    return pl.pallas_call(
        functools.partial(_attn_kernel, tq=tq, lam_init=lam_init),
        name="attn",
        grid=(batch, ATTN_HEADS, nq),
        in_specs=[
            smem, smem,
            pl.BlockSpec((tq, ATTN_V_DIM), lambda b, h, i: (b * nq + i, h)),
            pl.BlockSpec((seq, ATTN_V_DIM), lambda b, h, i: (b, h)),
            pl.BlockSpec((seq, ATTN_V_DIM), lambda b, h, i: (b, h)),
            pl.BlockSpec((1, ATTN_V_DIM), lambda b, h, i: (0, 0)),
        ],
        out_specs=pl.BlockSpec((tq, ATTN_V_DIM), lambda b, h, i: (b * nq + i, h)),
        out_shape=jax.ShapeDtypeStruct(q.shape, BF16),
        compiler_params=pltpu.CompilerParams(
            dimension_semantics=("parallel", "parallel", "parallel"),
            vmem_limit_bytes=48 * 1024 * 1024),
    )(lam1, slopes, q, k, v, sg)


def _mixout_kernel(prev_ref, main_ref, next_ref, attn_ref, x_ref, mod_ref, cw_ref, cb_ref, lg_ref, lb_ref,
                   wo_ref, x1_ref, win_ref, *, tiles_per_seq):
    T = main_ref.shape[0]
    j = lax.rem(pl.program_id(0), tiles_per_seq)
    H = CONV_HALO
    win_ref[0:H, :] = jnp.where(j == 0, 0.0, prev_ref[...])
    win_ref[H:H + T, :] = main_ref[...]
    win_ref[H + T:H + T + H, :] = jnp.where(j == tiles_per_seq - 1, 0.0, next_ref[...])
    acc = jnp.broadcast_to(cb_ref[...], (T, CONV_CH))
    off = H - CONV_KERNEL // 2
    for tap in range(CONV_KERNEL):
        acc = acc + cw_ref[tap:tap + 1, :] * win_ref[off + tap:off + tap + T, :]
    mu = jnp.mean(acc, axis=-1, keepdims=True)
    xc = acc - mu
    var = jnp.mean(xc * xc, axis=-1, keepdims=True)
    y = (xc * lax.rsqrt(var + EPS)) * lg_ref[...] + lb_ref[...]
    y = y * _sigmoid(y)
    mix = jnp.dot(y.astype(BF16), wo_ref[0:CONV_CH, :], preferred_element_type=F32)
    mix = mix + jnp.dot(attn_ref[...], wo_ref[CONV_CH:CONV_CH + ATTN_WIDTH, :], preferred_element_type=F32)
    x1_ref[...] = x_ref[...] + mod_ref[2:3, :] * mix


def _mixout(hglu, attn, x2d, mod3, cw, cb, lg, lb, wo_b, seq):
    N, D = x2d.shape
    T = TOK_TILE
    tiles_per_seq = seq // T
    hpt = T // CONV_HALO
    n_halo = N // CONV_HALO
    row = lambda i: (i, 0)
    const = lambda i: (0, 0)
    return pl.pallas_call(
        functools.partial(_mixout_kernel, tiles_per_seq=tiles_per_seq),
        name="mixout",
        grid=(N // T,),
        in_specs=[
            pl.BlockSpec((CONV_HALO, CONV_CH), lambda i: (jnp.maximum(i * hpt - 1, 0), 0)),
            pl.BlockSpec((T, CONV_CH), row),
            pl.BlockSpec((CONV_HALO, CONV_CH), lambda i: (jnp.minimum((i + 1) * hpt, n_halo - 1), 0)),
            pl.BlockSpec((T, ATTN_WIDTH), row),
            pl.BlockSpec((T, D), row),
            pl.BlockSpec((None, N_MOD, D), lambda i: (i // tiles_per_seq, 0, 0)),
            pl.BlockSpec(cw.shape, const),
            pl.BlockSpec((1, CONV_CH), const),
            pl.BlockSpec((1, CONV_CH), const),
            pl.BlockSpec((1, CONV_CH), const),
            pl.BlockSpec(wo_b.shape, const),
        ],
        out_specs=pl.BlockSpec((T, D), row),
        out_shape=jax.ShapeDtypeStruct((N, D), F32),
        scratch_shapes=[pltpu.VMEM((T + 2 * CONV_HALO, CONV_CH), F32)],
        compiler_params=pltpu.CompilerParams(
            dimension_semantics=("parallel",), vmem_limit_bytes=48 * 1024 * 1024),
    )(hglu, hglu, hglu, attn, x2d, mod3, cw, cb, lg, lb, wo_b)


def _route_kernel(x1_ref, mod_ref, g2_ref, wr_ref, br_ref, xs_ref, rt_ref, ce_ref, *, n_tiles):
    i = pl.program_id(0)

    @pl.when(i < n_tiles)
    def _():
        _route_tile(x1_ref, mod_ref, g2_ref, wr_ref, br_ref, xs_ref, rt_ref, ce_ref)

    @pl.when(i >= n_tiles)
    def _():
        xs_ref[...] = jnp.zeros(xs_ref.shape, F32)
        rt_ref[...] = jnp.zeros(rt_ref.shape, F32)
        ce_ref[...] = jnp.full(ce_ref.shape, N_EXPERTS, I32)


def _route_tile(x1_ref, mod_ref, g2_ref, wr_ref, br_ref, xs_ref, rt_ref, ce_ref):
    T = x1_ref.shape[0]
    E = N_EXPERTS
    L = xs_ref.shape[0]
    x1 = x1_ref[...]
    ms = jnp.mean(x1 * x1, axis=-1, keepdims=True)
    h2 = ((x1 * lax.rsqrt(ms + EPS)) * g2_ref[...]) * (1.0 + mod_ref[4:5, :]) + mod_ref[3:4, :]
    logits = lax.dot_general(wr_ref[...], h2, _nt_dims(), precision=HIGHEST,
                             preferred_element_type=F32) + br_ref[...]
    e_iota = lax.broadcasted_iota(I32, (E, T), 0)
    sels, tops = [], []
    l = logits
    for _ in range(TOP_K):
        m = jnp.max(l, axis=0, keepdims=True)
        idx = jnp.min(jnp.where(l == m, e_iota, E), axis=0, keepdims=True)
        sel = e_iota == idx
        l = jnp.where(sel, -jnp.inf, l)
        sels.append(sel)
        tops.append(m)
    ws = [jnp.exp(m - tops[0]) for m in tops]
    den = ws[0] + ws[1] + ws[2] + ws[3]
    gates = [w / den for w in ws]
    multi = jnp.zeros((E, T), F32)
    for sel in sels:
        multi = multi + jnp.where(sel, 1.0, 0.0)
    r_iota = lax.broadcasted_iota(I32, (T, T + V7X_LANES), 0)
    c_iota = lax.broadcasted_iota(I32, (T, T + V7X_LANES), 1)
    tri = jnp.where((r_iota < c_iota) | (c_iota >= T), 1.0, 0.0).astype(BF16)
    rk = jnp.dot(multi.astype(BF16), tri, preferred_element_type=F32)
    rank = rk[:, :T]
    cnt = rk[:, T:T + V7X_LANES]
    n8 = jnp.floor((cnt + (CHUNK - 1)) * (1.0 / CHUNK))
    e_iota_l = lax.broadcasted_iota(I32, (E, V7X_LANES), 0)
    lo8 = jnp.zeros((E, V7X_LANES), F32)
    for e in range(E - 1):
        lo8 = lo8 + jnp.where(e_iota_l > e, n8[e:e + 1, :], 0.0)
    base = lo8[:, 0:1] * float(CHUNK) + rank
    dests = [jnp.sum(jnp.where(sel, base, 0.0), axis=0, keepdims=True) for sel in sels]
    for k in range(TOP_K):
        rt_ref[k:k + 1, :] = dests[k]
        rt_ref[TOP_K + k:TOP_K + k + 1, :] = gates[k]
    end8 = lo8[:, 0:1] + n8[:, 0:1]
    s_iota = lax.broadcasted_iota(I32, (E, ce_ref.shape[1]), 1).astype(F32)
    ce_ref[...] = jnp.sum(jnp.where(end8 <= s_iota, 1.0, 0.0), axis=0, keepdims=True).astype(I32)
    j_iota = lax.broadcasted_iota(I32, (L, T), 0).astype(F32)
    pm = jnp.zeros((L, T), F32)
    for d in dests:
        pm = pm + jnp.where(j_iota == d, 1.0, 0.0)
    xs_ref[...] = jnp.dot(pm.astype(BF16), h2.astype(BF16), preferred_element_type=F32)


def _route(x1, mod3, g2, wr_t, br, seq, trash_tiles):
    N, D = x1.shape
    T = TOK_TILE
    tiles_per_seq = seq // T
    n_tiles = N // T
    n_ext = n_tiles + trash_tiles
    L = SORT_ROWS
    ce_w = _round_up(SORT_CHUNKS, V7X_LANES)
    const = lambda i: (0, 0)
    real = lambda i: jnp.minimum(i, n_tiles - 1)
    return pl.pallas_call(
        functools.partial(_route_kernel, n_tiles=n_tiles),
        name="route",
        grid=(n_ext,),
        in_specs=[
            pl.BlockSpec((T, D), lambda i: (real(i), 0)),
            pl.BlockSpec((None, N_MOD, D), lambda i: (real(i) // tiles_per_seq, 0, 0)),
            pl.BlockSpec((1, D), const),
            pl.BlockSpec(wr_t.shape, const),
            pl.BlockSpec(br.shape, const),
        ],
        out_specs=[
            pl.BlockSpec((L, D), lambda i: (i, 0)),
            pl.BlockSpec((None, 2 * TOP_K, T), lambda i: (i, 0, 0)),
            pl.BlockSpec((None, 1, ce_w), lambda i: (i, 0, 0)),
        ],
        out_shape=[
            jax.ShapeDtypeStruct((n_ext * L, D), F32),
            jax.ShapeDtypeStruct((n_ext, 2 * TOP_K, T), F32),
            jax.ShapeDtypeStruct((n_ext, 1, ce_w), I32),
        ],
        compiler_params=pltpu.CompilerParams(
            dimension_semantics=("parallel",), vmem_limit_bytes=48 * 1024 * 1024),
    )(x1, mod3, g2, wr_t, br)


def _expert_kernel(be_ref, nb_ref, slot_ref, xs_hbm, wgu_ref, bgu_ref, wd_ref, bd_ref, ys_hbm,
                   xbuf, obuf, sem_in, sem_out, *, dummy_chunk, trash_chunk0):
    del be_ref
    b = pl.program_id(0)
    nb = nb_ref[0]
    dff = wd_ref.shape[0]

    def rows_of(chunk):
        return pl.ds(pl.multiple_of(chunk * CHUNK, CHUNK), CHUNK)

    def gather_copy(blk, slot, j):
        s = slot_ref[blk * CHUNKS_PER_BLOCK + j]
        src = jnp.where(s < 0, dummy_chunk, s)
        return pltpu.make_async_copy(xs_hbm.at[rows_of(src)], xbuf.at[slot, pl.ds(j * CHUNK, CHUNK)],
                                     sem_in.at[slot])

    def scatter_copy(blk, slot, j):
        s = slot_ref[blk * CHUNKS_PER_BLOCK + j]
        dst = jnp.where(s < 0, trash_chunk0 - 1 - s, s)
        return pltpu.make_async_copy(obuf.at[slot, pl.ds(j * CHUNK, CHUNK)], ys_hbm.at[rows_of(dst)],
                                     sem_out.at[slot])

    def start_gather(blk, slot):
        for j in range(CHUNKS_PER_BLOCK):
            gather_copy(blk, slot, j).start()

    def wait_gather(blk, slot):
        for j in range(CHUNKS_PER_BLOCK):
            gather_copy(blk, slot, j).wait()

    def start_scatter(blk, slot):
        for j in range(CHUNKS_PER_BLOCK):
            scatter_copy(blk, slot, j).start()

    def wait_scatter(blk, slot):
        for j in range(CHUNKS_PER_BLOCK):
            scatter_copy(blk, slot, j).wait()

    @pl.when(b < nb)
    def _():
        slot = lax.rem(b, 2)

        @pl.when(b == 0)
        def _():
            start_gather(0, 0)

        @pl.when(b + 1 < nb)
        def _():
            start_gather(b + 1, 1 - slot)

        wait_gather(b, slot)

        @pl.when(b >= 2)
        def _():
            wait_scatter(b - 2, slot)

        xb = xbuf[slot].astype(BF16)
        gu = jnp.dot(xb, wgu_ref[...], preferred_element_type=F32) + bgu_ref[...]
        g = jnp.minimum(gu[:, :dff], SWIGLU_LIMIT)
        u = jnp.clip(gu[:, dff:], -SWIGLU_LIMIT, SWIGLU_LIMIT)
        y = (u + 1.0) * (g * _sigmoid(SWIGLU_ALPHA * g))
        obuf[slot] = jnp.dot(y.astype(BF16), wd_ref[...], preferred_element_type=F32) + bd_ref[...]
        start_scatter(b, slot)

        @pl.when(b == nb - 1)
        def _():
            wait_scatter(b, slot)

            @pl.when(b >= 1)
            def _():
                wait_scatter(b - 1, 1 - slot)


def _experts(blk_expert, nb_total, slots, xs, wgu_b, bgu, wd_b, bd, dummy_chunk, trash_chunk0):
    nb_max = blk_expert.shape[0]
    D = xs.shape[1]
    f2 = wgu_b.shape[2]
    dff = wd_b.shape[1]
    wmap = lambda b, be, nb, sl: (be[b], 0, 0)
    grid_spec = pltpu.PrefetchScalarGridSpec(
        num_scalar_prefetch=3,
        grid=(nb_max,),
        in_specs=[
            pl.BlockSpec(memory_space=pl.ANY),
            pl.BlockSpec((None, D, f2), wmap),
            pl.BlockSpec((None, 1, f2), wmap),
            pl.BlockSpec((None, dff, D), wmap),
            pl.BlockSpec((None, 1, D), wmap),
        ],
        out_specs=pl.BlockSpec(memory_space=pl.ANY),
        scratch_shapes=[
            pltpu.VMEM((2, BLOCK_ROWS, D), F32),
            pltpu.VMEM((2, BLOCK_ROWS, D), F32),
            pltpu.SemaphoreType.DMA((2,)),
            pltpu.SemaphoreType.DMA((2,)),
        ],
    )
    return pl.pallas_call(
        functools.partial(_expert_kernel, dummy_chunk=dummy_chunk, trash_chunk0=trash_chunk0),
        name="experts",
        grid_spec=grid_spec,
        out_shape=jax.ShapeDtypeStruct(xs.shape, F32),
        input_output_aliases={3: 0},
        compiler_params=pltpu.CompilerParams(
            dimension_semantics=("arbitrary",), vmem_limit_bytes=48 * 1024 * 1024),
    )(blk_expert, nb_total, slots, xs, wgu_b, bgu, wd_b, bd)


def _combine_kernel(ys_ref, rt_ref, x1_ref, mod_ref, x2_ref):
    T = x1_ref.shape[0]
    L = ys_ref.shape[0]
    j_iota = lax.broadcasted_iota(I32, (T, L), 1).astype(F32)
    wm = jnp.zeros((T, L), F32)
    for k in range(TOP_K):
        wm = wm + jnp.where(j_iota == rt_ref[:, k:k + 1], rt_ref[:, TOP_K + k:TOP_K + k + 1], 0.0)
    y = jnp.dot(wm.astype(BF16), ys_ref[...].astype(BF16), preferred_element_type=F32)
    x2_ref[...] = x1_ref[...] + mod_ref[5:6, :] * y


def _combine(ys, rt, x1, mod3, seq):
    N, D = x1.shape
    T = TOK_TILE
    tiles_per_seq = seq // T
    L = SORT_ROWS
    return pl.pallas_call(
        _combine_kernel,
        name="combine",
        grid=(N // T,),
        in_specs=[
            pl.BlockSpec((L, D), lambda i: (i, 0)),
            pl.BlockSpec((T, 2 * TOP_K), lambda i: (i, 0)),
            pl.BlockSpec((T, D), lambda i: (i, 0)),
            pl.BlockSpec((None, N_MOD, D), lambda i: (i // tiles_per_seq, 0, 0)),
        ],
        out_specs=pl.BlockSpec((T, D), lambda i: (i, 0)),
        out_shape=jax.ShapeDtypeStruct((N, D), F32),
        compiler_params=pltpu.CompilerParams(
            dimension_semantics=("parallel",), vmem_limit_bytes=48 * 1024 * 1024),
    )(ys, rt, x1, mod3)


def _block_tables(chunk_expert, n_tiles):
    E, cpb = N_EXPERTS, CHUNKS_PER_BLOCK
    keys = chunk_expert[:, 0, :SORT_CHUNKS].reshape(-1)
    order = jnp.argsort(keys, stable=True).astype(I32)
    cc = jnp.sum((keys[:, None] == jnp.arange(E, dtype=I32)[None, :]).astype(I32), axis=0)
    cend = jnp.cumsum(cc)
    cstart = cend - cc
    nbk = (cc + cpb - 1) // cpb
    bend = jnp.cumsum(nbk)
    bstart = bend - nbk
    nb_total = bend[-1:]
    nb_max = -(-(n_tiles * USED_CHUNKS_MAX) // cpb) + E
    bidx = jnp.arange(nb_max, dtype=I32)
    be = jnp.minimum(jnp.searchsorted(bend, bidx, side="right"), E - 1).astype(I32)
    j = jnp.arange(cpb, dtype=I32)[None, :]
    pos = cstart[be][:, None] + (bidx - bstart[be])[:, None] * cpb + j
    valid = (pos < cend[be][:, None]) & (bidx < nb_total[0])[:, None]
    src = order[jnp.clip(pos, 0, order.shape[0] - 1)]
    trash = be[:, None] * cpb + j
    slots = jnp.where(valid, src, -(trash + 1)).astype(I32)
    return be, nb_total.astype(I32), slots.reshape(-1)


def kernel(x, c, w_ada, b_ada, norm1_g, w_in, q_norm_g, k_norm_g, lambda_q1, lambda_k1, lambda_q2,
           lambda_k2, subln_g, conv_w, conv_b, conv_ln_g, conv_ln_b, w_out, norm2_g, w_router, b_router,
           w_gate_up, b_gate_up, w_down, b_down):
    B, S, D = x.shape
    N = B * S
    assert S % INPROJ_ROWS == 0 and S % ATTN_Q_ROWS == 0 and S % TOK_TILE == 0
    n_tiles = N // TOK_TILE
    trash_tiles = -(-(N_EXPERTS * CHUNKS_PER_BLOCK) // SORT_CHUNKS)
    dummy_chunk = SORT_CHUNKS - 1
    trash_chunk0 = n_tiles * SORT_CHUNKS
    slopes = jnp.exp2(-8.0 * jnp.arange(1, ATTN_HEADS + 1, dtype=F32) / ATTN_HEADS)
    xt = x.reshape(N, D)
    for l in range(w_ada.shape[0]):
        lam_init = 0.8 - 0.6 * math.exp(-0.3 * l)
        row = lambda a: a[l][None, :]
        mod, lam = _ada(c, w_ada[l], row(b_ada), row(lambda_q1), row(lambda_k1), row(lambda_q2),
                        row(lambda_k2), lam_init)
        mod3 = mod.reshape(B, N_MOD, D)
        qg = jnp.tile(q_norm_g[l], 2 * ATTN_HEADS)[None, :] * (ATTN_HEAD_DIM ** -0.5)
        kg = jnp.tile(k_norm_g[l], 2 * ATTN_HEADS)[None, :]
        hglu, q, k, v = _inproj(xt, mod3, row(norm1_g), w_in[l].astype(BF16), qg, kg, S)
        attn = _attention(lam[0, 0:1], slopes, q, k, v, row(subln_g), B, S, lam_init)
        cw = jnp.pad(conv_w[l], ((0, 1), (0, 0)))
        x1 = _mixout(hglu, attn, xt, mod3, cw, row(conv_b), row(conv_ln_g), row(conv_ln_b),
                     w_out[l].astype(BF16), S)
        xs, rt, ce = _route(x1, mod3, row(norm2_g), w_router[l].T, b_router[l][:, None], S, trash_tiles)
        be, nb_total, slots = _block_tables(ce[:n_tiles], n_tiles)
        ys = _experts(be, nb_total, slots, xs, w_gate_up[l].astype(BF16), b_gate_up[l][:, None, :],
                      w_down[l].astype(BF16), b_down[l][:, None, :], dummy_chunk, trash_chunk0)
        rt_tok = rt[:n_tiles].transpose(0, 2, 1).reshape(N, 2 * TOP_K)
        xt = _combine(ys, rt_tok, x1, mod3, S)
    return xt.reshape(B, S, D)
```

```python
import functools
import math

import jax
import jax.numpy as jnp
from jax import lax
from jax.experimental import pallas as pl
from jax.experimental.pallas import tpu as pltpu

F32 = jnp.float32
BF16 = jnp.bfloat16
I32 = jnp.int32
HIGHEST = lax.Precision.HIGHEST

CONV_CH = 512
CONV_KERNEL = 31
CONV_HALO = 16
ATTN_HEADS = 4
ATTN_HEAD_DIM = 64
ATTN_V_DIM = 2 * ATTN_HEAD_DIM
ATTN_WIDTH = ATTN_HEADS * ATTN_V_DIM
N_EXPERTS = 32
TOP_K = 4
SWIGLU_LIMIT = 7.0
SWIGLU_ALPHA = 1.702
N_MOD = 6
EPS = 1e-5

V7X_LANES = 128
V7X_SUBLANES = 8

INPROJ_ROWS = 512
ATTN_Q_ROWS = 512
ATTN_SUB_ROWS = 256
TOK_TILE = 256
CHUNK = V7X_SUBLANES
BLOCK_ROWS = 512
CHUNKS_PER_BLOCK = BLOCK_ROWS // CHUNK


def _round_up(a, b):
    return (a + b - 1) // b * b


SORT_ROWS = _round_up(TOP_K * TOK_TILE + N_EXPERTS * (CHUNK - 1) + CHUNK, V7X_LANES)
SORT_CHUNKS = SORT_ROWS // CHUNK
USED_CHUNKS_MAX = (TOP_K * TOK_TILE + N_EXPERTS * (CHUNK - 1)) // CHUNK


def _sigmoid(v):
    return 1.0 / (1.0 + jnp.exp(-v))


def _nt_dims():
    return (((1,), (1,)), ((), ()))


def _ada_kernel(c_ref, w_ref, b_ref, lq1_ref, lk1_ref, lq2_ref, lk2_ref, mod_ref, lam_ref, *, lam_init):
    c = c_ref[...]
    sc = c * _sigmoid(c)
    mod_ref[...] = jnp.dot(sc, w_ref[...], precision=HIGHEST, preferred_element_type=F32) + b_ref[...]
    s1 = jnp.sum(lq1_ref[...] * lk1_ref[...], axis=-1, keepdims=True)
    s2 = jnp.sum(lq2_ref[...] * lk2_ref[...], axis=-1, keepdims=True)
    lam = jnp.exp(s1) - jnp.exp(s2) + lam_init
    lam_ref[...] = jnp.broadcast_to(lam, lam_ref.shape)


def _ada(c, w_ada, b_ada, lq1, lk1, lq2, lk2, lam_init):
    B, D = c.shape
    cols = w_ada.shape[1]
    bc = D
    vec = pl.BlockSpec((1, ATTN_HEAD_DIM), lambda j: (0, 0))
    return pl.pallas_call(
        functools.partial(_ada_kernel, lam_init=lam_init),
        name="ada",
        grid=(cols // bc,),
        in_specs=[
            pl.BlockSpec((B, D), lambda j: (0, 0)),
            pl.BlockSpec((D, bc), lambda j: (0, j)),
            pl.BlockSpec((1, bc), lambda j: (0, j)),
            vec, vec, vec, vec,
        ],
        out_specs=[
            pl.BlockSpec((B, bc), lambda j: (0, j)),
            pl.BlockSpec((V7X_SUBLANES, V7X_LANES), lambda j: (0, 0)),
        ],
        out_shape=[
            jax.ShapeDtypeStruct((B, cols), F32),
            jax.ShapeDtypeStruct((V7X_SUBLANES, V7X_LANES), F32),
        ],
        compiler_params=pltpu.CompilerParams(dimension_semantics=("arbitrary",)),
    )(c, w_ada, b_ada, lq1, lk1, lq2, lk2)


def _group_rms_scale(t, lo):
    sq = t * t
    s1 = jnp.sum(jnp.where(lo, sq, 0.0), axis=-1, keepdims=True)
    s2 = jnp.sum(jnp.where(lo, 0.0, sq), axis=-1, keepdims=True)
    r1 = lax.rsqrt(s1 * (1.0 / ATTN_HEAD_DIM) + EPS)
    r2 = lax.rsqrt(s2 * (1.0 / ATTN_HEAD_DIM) + EPS)
    return jnp.where(lo, r1, r2)


def _inproj_kernel(x_ref, mod_ref, g1_ref, w_ref, qg_ref, kg_ref, hglu_ref, q_ref, k_ref, v_ref):
    x = x_ref[...]
    ms = jnp.mean(x * x, axis=-1, keepdims=True)
    shift = mod_ref[0:1, :]
    scale = mod_ref[1:2, :]
    h = ((x * lax.rsqrt(ms + EPS)) * g1_ref[...]) * (1.0 + scale) + shift
    hb = h.astype(BF16)
    c2 = 2 * CONV_CH
    ag = jnp.dot(hb, w_ref[:, 0:c2], preferred_element_type=F32)
    hglu_ref[...] = ag[:, :CONV_CH] * _sigmoid(ag[:, CONV_CH:])
    lo = lax.broadcasted_iota(I32, (1, V7X_LANES), 1) < ATTN_HEAD_DIM
    for src_col, g_ref, o_ref in ((c2, qg_ref, q_ref), (c2 + ATTN_WIDTH, kg_ref, k_ref)):
        t = jnp.dot(hb, w_ref[:, src_col:src_col + ATTN_WIDTH], preferred_element_type=F32)
        for hd in range(ATTN_HEADS):
            sl = slice(hd * V7X_LANES, (hd + 1) * V7X_LANES)
            th = t[:, sl]
            o_ref[:, sl] = (th * _group_rms_scale(th, lo) * g_ref[:, sl]).astype(BF16)
    v0 = c2 + 2 * ATTN_WIDTH
    v_ref[...] = jnp.dot(hb, w_ref[:, v0:v0 + ATTN_WIDTH], preferred_element_type=F32).astype(BF16)


def _inproj(x2d, mod3, g1, w_in_b, qg, kg, seq):
    N, D = x2d.shape
    tm = INPROJ_ROWS
    steps_per_seq = seq // tm
    cols = w_in_b.shape[1]
    row = lambda i: (i, 0)
    const = lambda i: (0, 0)
    return pl.pallas_call(
        _inproj_kernel,
        name="inproj",
        grid=(N // tm,),
        in_specs=[
            pl.BlockSpec((tm, D), row),
            pl.BlockSpec((None, N_MOD, D), lambda i: (i // steps_per_seq, 0, 0)),
            pl.BlockSpec((1, D), const),
            pl.BlockSpec((D, cols), const),
            pl.BlockSpec((1, ATTN_WIDTH), const),
            pl.BlockSpec((1, ATTN_WIDTH), const),
        ],
        out_specs=[
            pl.BlockSpec((tm, CONV_CH), row),
            pl.BlockSpec((tm, ATTN_WIDTH), row),
            pl.BlockSpec((tm, ATTN_WIDTH), row),
            pl.BlockSpec((tm, ATTN_WIDTH), row),
        ],
        out_shape=[
            jax.ShapeDtypeStruct((N, CONV_CH), F32),
            jax.ShapeDtypeStruct((N, ATTN_WIDTH), BF16),
            jax.ShapeDtypeStruct((N, ATTN_WIDTH), BF16),
            jax.ShapeDtypeStruct((N, ATTN_WIDTH), BF16),
        ],
        compiler_params=pltpu.CompilerParams(
            dimension_semantics=("parallel",), vmem_limit_bytes=48 * 1024 * 1024),
    )(x2d, mod3, g1, w_in_b, qg, kg)


def _attn_kernel(lam_ref, slope_ref, q_ref, k_ref, v_ref, sg_ref, o_ref, *, tq, ts, lam_init):
    hd = pl.program_id(1)
    i = pl.program_id(2)
    lam = lam_ref[0]
    slope = slope_ref[hd]
    k = k_ref[...]
    v = v_ref[...]
    S = k.shape[0]
    lo = lax.broadcasted_iota(I32, (1, V7X_LANES), 1) < ATTN_HEAD_DIM
    kpos = lax.broadcasted_iota(I32, (1, S), 1).astype(F32) * slope

    def softmax_parts(qc, bias):
        s = lax.dot_general(qc, k, _nt_dims(), preferred_element_type=F32) - bias
        m = jnp.max(s, axis=-1, keepdims=True)
        p = jnp.exp2(s - m)
        return p, jnp.sum(p, axis=-1, keepdims=True)

    for sub in range(tq // ts):
        rows = slice(sub * ts, (sub + 1) * ts)
        q = q_ref[rows, :]
        zero = jnp.zeros_like(q)
        qpos = (i * tq + sub * ts + lax.broadcasted_iota(I32, (ts, 1), 0)).astype(F32) * slope
        bias = jnp.abs(kpos - qpos)
        p1, l1 = softmax_parts(jnp.where(lo, q, zero), bias)
        p2, l2 = softmax_parts(jnp.where(lo, zero, q), bias)
        p = p1 - p2 * (lam * l1 / l2)
        o = jnp.dot(p.astype(BF16), v, preferred_element_type=F32) * (1.0 / l1)
        ms = jnp.mean(o * o, axis=-1, keepdims=True)
        o = (o * lax.rsqrt(ms + EPS)) * sg_ref[...] * (1.0 - lam_init)
        o_ref[rows, :] = o.astype(BF16)


def _attention(lam1, slopes, q, k, v, sg, batch, seq, lam_init):
    tq = ATTN_Q_ROWS
    nq = seq // tq
    smem = pl.BlockSpec(memory_space=pltpu.SMEM)
    return pl.pallas_call(
        functools.partial(_attn_kernel, tq=tq, ts=ATTN_SUB_ROWS, lam_init=lam_init),
        name="attn",
        grid=(batch, ATTN_HEADS, nq),
        in_specs=[
            smem, smem,
            pl.BlockSpec((tq, ATTN_V_DIM), lambda b, h, i: (b * nq + i, h)),
            pl.BlockSpec((seq, ATTN_V_DIM), lambda b, h, i: (b, h)),
            pl.BlockSpec((seq, ATTN_V_DIM), lambda b, h, i: (b, h)),
            pl.BlockSpec((1, ATTN_V_DIM), lambda b, h, i: (0, 0)),
        ],
        out_specs=pl.BlockSpec((tq, ATTN_V_DIM), lambda b, h, i: (b * nq + i, h)),
        out_shape=jax.ShapeDtypeStruct(q.shape, BF16),
        compiler_params=pltpu.CompilerParams(
            dimension_semantics=("parallel", "parallel", "parallel"),
            vmem_limit_bytes=48 * 1024 * 1024),
    )(lam1, slopes, q, k, v, sg)


def _mixout_kernel(prev_ref, main_ref, next_ref, attn_ref, x_ref, mod_ref, cw_ref, cb_ref, lg_ref, lb_ref,
                   wo_ref, x1_ref, win_ref, sh_ref, *, tiles_per_seq):
    T = main_ref.shape[0]
    j = lax.rem(pl.program_id(0), tiles_per_seq)
    H = CONV_HALO
    win_ref[0:H, :] = jnp.where(j == 0, 0.0, prev_ref[...])
    win_ref[H:H + T, :] = main_ref[...]
    win_ref[H + T:H + T + H, :] = jnp.where(j == tiles_per_seq - 1, 0.0, next_ref[...])
    rows = sh_ref.shape[1]
    for r in range(1, V7X_SUBLANES):
        sh_ref[r] = win_ref[r:r + rows, :]
    acc = jnp.broadcast_to(cb_ref[...], (T, CONV_CH))
    off = H - CONV_KERNEL // 2
    for tap in range(CONV_KERNEL):
        a, r = divmod(off + tap, V7X_SUBLANES)
        src = win_ref if r == 0 else sh_ref.at[r]
        acc = acc + cw_ref[tap:tap + 1, :] * src[a * V7X_SUBLANES:a * V7X_SUBLANES + T, :]
    mu = jnp.mean(acc, axis=-1, keepdims=True)
    xc = acc - mu
    var = jnp.mean(xc * xc, axis=-1, keepdims=True)
    y = (xc * lax.rsqrt(var + EPS)) * lg_ref[...] + lb_ref[...]
    y = y * _sigmoid(y)
    mix = jnp.dot(y.astype(BF16), wo_ref[0:CONV_CH, :], preferred_element_type=F32)
    mix = mix + jnp.dot(attn_ref[...], wo_ref[CONV_CH:CONV_CH + ATTN_WIDTH, :], preferred_element_type=F32)
    x1_ref[...] = x_ref[...] + mod_ref[2:3, :] * mix


def _mixout(hglu, attn, x2d, mod3, cw, cb, lg, lb, wo_b, seq):
    N, D = x2d.shape
    T = TOK_TILE
    tiles_per_seq = seq // T
    hpt = T // CONV_HALO
    n_halo = N // CONV_HALO
    row = lambda i: (i, 0)
    const = lambda i: (0, 0)
    return pl.pallas_call(
        functools.partial(_mixout_kernel, tiles_per_seq=tiles_per_seq),
        name="mixout",
        grid=(N // T,),
        in_specs=[
            pl.BlockSpec((CONV_HALO, CONV_CH), lambda i: (jnp.maximum(i * hpt - 1, 0), 0)),
            pl.BlockSpec((T, CONV_CH), row),
            pl.BlockSpec((CONV_HALO, CONV_CH), lambda i: (jnp.minimum((i + 1) * hpt, n_halo - 1), 0)),
            pl.BlockSpec((T, ATTN_WIDTH), row),
            pl.BlockSpec((T, D), row),
            pl.BlockSpec((None, N_MOD, D), lambda i: (i // tiles_per_seq, 0, 0)),
            pl.BlockSpec(cw.shape, const),
            pl.BlockSpec((1, CONV_CH), const),
            pl.BlockSpec((1, CONV_CH), const),
            pl.BlockSpec((1, CONV_CH), const),
            pl.BlockSpec(wo_b.shape, const),
        ],
        out_specs=pl.BlockSpec((T, D), row),
        out_shape=jax.ShapeDtypeStruct((N, D), F32),
        scratch_shapes=[
            pltpu.VMEM((T + 2 * CONV_HALO, CONV_CH), F32),
            pltpu.VMEM((V7X_SUBLANES, T + 2 * CONV_HALO - V7X_SUBLANES, CONV_CH), F32),
        ],
        compiler_params=pltpu.CompilerParams(
            dimension_semantics=("parallel",), vmem_limit_bytes=48 * 1024 * 1024),
    )(hglu, hglu, hglu, attn, x2d, mod3, cw, cb, lg, lb, wo_b)


def _route_kernel(x1_ref, mod_ref, g2_ref, wr_ref, br_ref, xs_ref, rt_ref, ce_ref, *, n_tiles):
    i = pl.program_id(0)

    @pl.when(i < n_tiles)
    def _():
        _route_tile(x1_ref, mod_ref, g2_ref, wr_ref, br_ref, xs_ref, rt_ref, ce_ref)

    @pl.when(i >= n_tiles)
    def _():
        xs_ref[...] = jnp.zeros(xs_ref.shape, F32)
        rt_ref[...] = jnp.zeros(rt_ref.shape, F32)
        ce_ref[...] = jnp.full(ce_ref.shape, N_EXPERTS, I32)


def _route_tile(x1_ref, mod_ref, g2_ref, wr_ref, br_ref, xs_ref, rt_ref, ce_ref):
    T = x1_ref.shape[0]
    E = N_EXPERTS
    L = xs_ref.shape[0]
    x1 = x1_ref[...]
    ms = jnp.mean(x1 * x1, axis=-1, keepdims=True)
    h2 = ((x1 * lax.rsqrt(ms + EPS)) * g2_ref[...]) * (1.0 + mod_ref[4:5, :]) + mod_ref[3:4, :]
    logits = lax.dot_general(wr_ref[...], h2, _nt_dims(), precision=HIGHEST,
                             preferred_element_type=F32) + br_ref[...]
    e_iota = lax.broadcasted_iota(I32, (E, T), 0)
    sels, tops = [], []
    l = logits
    for _ in range(TOP_K):
        m = jnp.max(l, axis=0, keepdims=True)
        idx = jnp.min(jnp.where(l == m, e_iota, E), axis=0, keepdims=True)
        sel = e_iota == idx
        l = jnp.where(sel, -jnp.inf, l)
        sels.append(sel)
        tops.append(m)
    ws = [jnp.exp(m - tops[0]) for m in tops]
    den = ws[0] + ws[1] + ws[2] + ws[3]
    gates = [w / den for w in ws]
    multi = jnp.zeros((E, T), F32)
    for sel in sels:
        multi = multi + jnp.where(sel, 1.0, 0.0)
    r_iota = lax.broadcasted_iota(I32, (T, T + V7X_LANES), 0)
    c_iota = lax.broadcasted_iota(I32, (T, T + V7X_LANES), 1)
    tri = jnp.where((r_iota < c_iota) | (c_iota >= T), 1.0, 0.0).astype(BF16)
    rk = jnp.dot(multi.astype(BF16), tri, preferred_element_type=F32)
    rank = rk[:, :T]
    cnt = rk[:, T:T + V7X_LANES]
    n8 = jnp.floor((cnt + (CHUNK - 1)) * (1.0 / CHUNK))
    e_iota_l = lax.broadcasted_iota(I32, (E, V7X_LANES), 0)
    lo8 = jnp.zeros((E, V7X_LANES), F32)
    for e in range(E - 1):
        lo8 = lo8 + jnp.where(e_iota_l > e, n8[e:e + 1, :], 0.0)
    base = lo8[:, 0:1] * float(CHUNK) + rank
    dests = [jnp.sum(jnp.where(sel, base, 0.0), axis=0, keepdims=True) for sel in sels]
    for k in range(TOP_K):
        rt_ref[k:k + 1, :] = dests[k]
        rt_ref[TOP_K + k:TOP_K + k + 1, :] = gates[k]
    end8 = lo8[:, 0:1] + n8[:, 0:1]
    s_iota = lax.broadcasted_iota(I32, (E, ce_ref.shape[1]), 1).astype(F32)
    ce_ref[...] = jnp.sum(jnp.where(end8 <= s_iota, 1.0, 0.0), axis=0, keepdims=True).astype(I32)
    j_iota = lax.broadcasted_iota(I32, (L, T), 0).astype(F32)
    pm = jnp.zeros((L, T), F32)
    for d in dests:
        pm = pm + jnp.where(j_iota == d, 1.0, 0.0)
    xs_ref[...] = jnp.dot(pm.astype(BF16), h2.astype(BF16), preferred_element_type=F32)


def _route(x1, mod3, g2, wr_t, br, seq, trash_tiles):
    N, D = x1.shape
    T = TOK_TILE
    tiles_per_seq = seq // T
    n_tiles = N // T
    n_ext = n_tiles + trash_tiles
    L = SORT_ROWS
    ce_w = _round_up(SORT_CHUNKS, V7X_LANES)
    const = lambda i: (0, 0)
    real = lambda i: jnp.minimum(i, n_tiles - 1)
    return pl.pallas_call(
        functools.partial(_route_kernel, n_tiles=n_tiles),
        name="route",
        grid=(n_ext,),
        in_specs=[
            pl.BlockSpec((T, D), lambda i: (real(i), 0)),
            pl.BlockSpec((None, N_MOD, D), lambda i: (real(i) // tiles_per_seq, 0, 0)),
            pl.BlockSpec((1, D), const),
            pl.BlockSpec(wr_t.shape, const),
            pl.BlockSpec(br.shape, const),
        ],
        out_specs=[
            pl.BlockSpec((L, D), lambda i: (i, 0)),
            pl.BlockSpec((None, 2 * TOP_K, T), lambda i: (i, 0, 0)),
            pl.BlockSpec((None, 1, ce_w), lambda i: (i, 0, 0)),
        ],
        out_shape=[
            jax.ShapeDtypeStruct((n_ext * L, D), F32),
            jax.ShapeDtypeStruct((n_ext, 2 * TOP_K, T), F32),
            jax.ShapeDtypeStruct((n_ext, 1, ce_w), I32),
        ],
        compiler_params=pltpu.CompilerParams(
            dimension_semantics=("parallel",), vmem_limit_bytes=48 * 1024 * 1024),
    )(x1, mod3, g2, wr_t, br)


def _expert_kernel(be_ref, nb_ref, slot_ref, xs_hbm, wgu_ref, bgu_ref, wd_ref, bd_ref, ys_hbm,
                   xbuf, obuf, wgu_b, wd_b, sem_in, sem_out, *, dummy_chunk, trash_chunk0):
    b = pl.program_id(0)
    nb = nb_ref[0]
    dff = wd_ref.shape[0]

    def rows_of(chunk):
        return pl.ds(pl.multiple_of(chunk * CHUNK, CHUNK), CHUNK)

    def gather_copy(blk, slot, j):
        s = slot_ref[blk * CHUNKS_PER_BLOCK + j]
        src = jnp.where(s < 0, dummy_chunk, s)
        return pltpu.make_async_copy(xs_hbm.at[rows_of(src)], xbuf.at[slot, pl.ds(j * CHUNK, CHUNK)],
                                     sem_in.at[slot])

    def scatter_copy(blk, slot, j):
        s = slot_ref[blk * CHUNKS_PER_BLOCK + j]
        dst = jnp.where(s < 0, trash_chunk0 - 1 - s, s)
        return pltpu.make_async_copy(obuf.at[slot, pl.ds(j * CHUNK, CHUNK)], ys_hbm.at[rows_of(dst)],
                                     sem_out.at[slot])

    def start_gather(blk, slot):
        for j in range(CHUNKS_PER_BLOCK):
            gather_copy(blk, slot, j).start()

    def wait_gather(blk, slot):
        for j in range(CHUNKS_PER_BLOCK):
            gather_copy(blk, slot, j).wait()

    def start_scatter(blk, slot):
        for j in range(CHUNKS_PER_BLOCK):
            scatter_copy(blk, slot, j).start()

    def wait_scatter(blk, slot):
        for j in range(CHUNKS_PER_BLOCK):
            scatter_copy(blk, slot, j).wait()

    @pl.when(b < nb)
    def _():
        slot = lax.rem(b, 2)

        @pl.when(b == 0)
        def _():
            start_gather(0, 0)

        @pl.when(b + 1 < nb)
        def _():
            start_gather(b + 1, 1 - slot)

        @pl.when((b == 0) | (be_ref[b] != be_ref[jnp.maximum(b - 1, 0)]))
        def _():
            wgu_b[...] = wgu_ref[...].astype(BF16)
            wd_b[...] = wd_ref[...].astype(BF16)

        wait_gather(b, slot)

        @pl.when(b >= 2)
        def _():
            wait_scatter(b - 2, slot)

        xb = xbuf[slot].astype(BF16)
        gu = jnp.dot(xb, wgu_b[...], preferred_element_type=F32) + bgu_ref[...]
        g = jnp.minimum(gu[:, :dff], SWIGLU_LIMIT)
        u = jnp.clip(gu[:, dff:], -SWIGLU_LIMIT, SWIGLU_LIMIT)
        y = (u + 1.0) * (g * _sigmoid(SWIGLU_ALPHA * g))
        obuf[slot] = jnp.dot(y.astype(BF16), wd_b[...], preferred_element_type=F32) + bd_ref[...]
        start_scatter(b, slot)

        @pl.when(b == nb - 1)
        def _():
            wait_scatter(b, slot)

            @pl.when(b >= 1)
            def _():
                wait_scatter(b - 1, 1 - slot)


def _experts(blk_expert, nb_total, slots, xs, wgu, bgu, wd, bd, dummy_chunk, trash_chunk0):
    nb_max = blk_expert.shape[0]
    D = xs.shape[1]
    f2 = wgu.shape[2]
    dff = wd.shape[1]
    wmap = lambda b, be, nb, sl: (be[b], 0, 0)
    grid_spec = pltpu.PrefetchScalarGridSpec(
        num_scalar_prefetch=3,
        grid=(nb_max,),
        in_specs=[
            pl.BlockSpec(memory_space=pl.ANY),
            pl.BlockSpec((None, D, f2), wmap),
            pl.BlockSpec((None, 1, f2), wmap),
            pl.BlockSpec((None, dff, D), wmap),
            pl.BlockSpec((None, 1, D), wmap),
        ],
        out_specs=pl.BlockSpec(memory_space=pl.ANY),
        scratch_shapes=[
            pltpu.VMEM((2, BLOCK_ROWS, D), F32),
            pltpu.VMEM((2, BLOCK_ROWS, D), F32),
            pltpu.VMEM((D, f2), BF16),
            pltpu.VMEM((dff, D), BF16),
            pltpu.SemaphoreType.DMA((2,)),
            pltpu.SemaphoreType.DMA((2,)),
        ],
    )
    return pl.pallas_call(
        functools.partial(_expert_kernel, dummy_chunk=dummy_chunk, trash_chunk0=trash_chunk0),
        name="experts",
        grid_spec=grid_spec,
        out_shape=jax.ShapeDtypeStruct(xs.shape, F32),
        input_output_aliases={3: 0},
        compiler_params=pltpu.CompilerParams(
            dimension_semantics=("arbitrary",), vmem_limit_bytes=56 * 1024 * 1024),
    )(blk_expert, nb_total, slots, xs, wgu, bgu, wd, bd)


def _combine_kernel(ys_ref, rt_ref, x1_ref, mod_ref, x2_ref):
    T = x1_ref.shape[0]
    L = ys_ref.shape[0]
    j_iota = lax.broadcasted_iota(I32, (T, L), 1).astype(F32)
    wm = jnp.zeros((T, L), F32)
    for k in range(TOP_K):
        wm = wm + jnp.where(j_iota == rt_ref[:, k:k + 1], rt_ref[:, TOP_K + k:TOP_K + k + 1], 0.0)
    y = jnp.dot(wm.astype(BF16), ys_ref[...].astype(BF16), preferred_element_type=F32)
    x2_ref[...] = x1_ref[...] + mod_ref[5:6, :] * y


def _combine(ys, rt, x1, mod3, seq):
    N, D = x1.shape
    T = TOK_TILE
    tiles_per_seq = seq // T
    L = SORT_ROWS
    return pl.pallas_call(
        _combine_kernel,
        name="combine",
        grid=(N // T,),
        in_specs=[
            pl.BlockSpec((L, D), lambda i: (i, 0)),
            pl.BlockSpec((T, 2 * TOP_K), lambda i: (i, 0)),
            pl.BlockSpec((T, D), lambda i: (i, 0)),
            pl.BlockSpec((None, N_MOD, D), lambda i: (i // tiles_per_seq, 0, 0)),
        ],
        out_specs=pl.BlockSpec((T, D), lambda i: (i, 0)),
        out_shape=jax.ShapeDtypeStruct((N, D), F32),
        compiler_params=pltpu.CompilerParams(
            dimension_semantics=("parallel",), vmem_limit_bytes=48 * 1024 * 1024),
    )(ys, rt, x1, mod3)


def _block_tables(chunk_expert, n_tiles):
    E, cpb = N_EXPERTS, CHUNKS_PER_BLOCK
    keys = chunk_expert[:, 0, :SORT_CHUNKS].reshape(-1)
    order = jnp.argsort(keys, stable=True).astype(I32)
    cc = jnp.sum((keys[:, None] == jnp.arange(E, dtype=I32)[None, :]).astype(I32), axis=0)
    cend = jnp.cumsum(cc)
    cstart = cend - cc
    nbk = (cc + cpb - 1) // cpb
    bend = jnp.cumsum(nbk)
    bstart = bend - nbk
    nb_total = bend[-1:]
    nb_max = -(-(n_tiles * USED_CHUNKS_MAX) // cpb) + E
    bidx = jnp.arange(nb_max, dtype=I32)
    be = jnp.minimum(jnp.sum((bidx[:, None] >= bend[None, :]).astype(I32), axis=1), E - 1)
    j = jnp.arange(cpb, dtype=I32)[None, :]
    pos = cstart[be][:, None] + (bidx - bstart[be])[:, None] * cpb + j
    valid = (pos < cend[be][:, None]) & (bidx < nb_total[0])[:, None]
    src = order[jnp.clip(pos, 0, order.shape[0] - 1)]
    trash = be[:, None] * cpb + j
    slots = jnp.where(valid, src, -(trash + 1)).astype(I32)
    return be, nb_total.astype(I32), slots.reshape(-1)


def kernel(x, c, w_ada, b_ada, norm1_g, w_in, q_norm_g, k_norm_g, lambda_q1, lambda_k1, lambda_q2,
           lambda_k2, subln_g, conv_w, conv_b, conv_ln_g, conv_ln_b, w_out, norm2_g, w_router, b_router,
           w_gate_up, b_gate_up, w_down, b_down):
    B, S, D = x.shape
    N = B * S
    assert S % INPROJ_ROWS == 0 and S % ATTN_Q_ROWS == 0 and S % TOK_TILE == 0
    n_tiles = N // TOK_TILE
    trash_tiles = -(-(N_EXPERTS * CHUNKS_PER_BLOCK) // SORT_CHUNKS)
    dummy_chunk = SORT_CHUNKS - 1
    trash_chunk0 = n_tiles * SORT_CHUNKS
    log2e = math.log2(math.e)
    slopes = jnp.exp2(-8.0 * jnp.arange(1, ATTN_HEADS + 1, dtype=F32) / ATTN_HEADS) * log2e
    xt = x.reshape(N, D)
    for l in range(w_ada.shape[0]):
        lam_init = 0.8 - 0.6 * math.exp(-0.3 * l)
        row = lambda a: a[l][None, :]
        mod, lam = _ada(c, w_ada[l], row(b_ada), row(lambda_q1), row(lambda_k1), row(lambda_q2),
                        row(lambda_k2), lam_init)
        mod3 = mod.reshape(B, N_MOD, D)
        qg = jnp.tile(q_norm_g[l], 2 * ATTN_HEADS)[None, :] * (ATTN_HEAD_DIM ** -0.5 * log2e)
        kg = jnp.tile(k_norm_g[l], 2 * ATTN_HEADS)[None, :]
        hglu, q, k, v = _inproj(xt, mod3, row(norm1_g), w_in[l].astype(BF16), qg, kg, S)
        attn = _attention(lam[0, 0:1], slopes, q, k, v, row(subln_g), B, S, lam_init)
        cw = jnp.pad(conv_w[l], ((0, 1), (0, 0)))
        x1 = _mixout(hglu, attn, xt, mod3, cw, row(conv_b), row(conv_ln_g), row(conv_ln_b),
                     w_out[l].astype(BF16), S)
        xs, rt, ce = _route(x1, mod3, row(norm2_g), w_router[l].T, b_router[l][:, None], S, trash_tiles)
        be, nb_total, slots = _block_tables(ce[:n_tiles], n_tiles)
        ys = _experts(be, nb_total, slots, xs, w_gate_up[l], b_gate_up[l][:, None, :],
                      w_down[l], b_down[l][:, None, :], dummy_chunk, trash_chunk0)
        rt_tok = rt[:n_tiles].transpose(0, 2, 1).reshape(N, 2 * TOP_K)
        xt = _combine(ys, rt_tok, x1, mod3, S)
    return xt.reshape(B, S, D)
```

```python
import functools
import math

import jax
import jax.numpy as jnp
from jax import lax
from jax.experimental import pallas as pl
from jax.experimental.pallas import tpu as pltpu

F32 = jnp.float32
BF16 = jnp.bfloat16
I32 = jnp.int32
U32 = jnp.uint32
HIGHEST = lax.Precision.HIGHEST

CONV_CH = 512
CONV_KERNEL = 31
CONV_HALO = 16
ATTN_HEADS = 4
ATTN_HEAD_DIM = 64
ATTN_V_DIM = 2 * ATTN_HEAD_DIM
ATTN_WIDTH = ATTN_HEADS * ATTN_V_DIM
N_EXPERTS = 32
TOP_K = 4
SWIGLU_LIMIT = 7.0
SWIGLU_ALPHA = 1.702
N_MOD = 6
EPS = 1e-5

V7X_LANES = 128
V7X_SUBLANES = 8

INPROJ_ROWS = 512
ATTN_Q_ROWS = 512
ATTN_KEY_BLOCK = 256
TOK_TILE = 256
CHUNK = V7X_SUBLANES
BLOCK_ROWS = 512
CHUNKS_PER_BLOCK = BLOCK_ROWS // CHUNK


def _round_up(a, b):
    return (a + b - 1) // b * b


SORT_ROWS = _round_up(TOP_K * TOK_TILE + N_EXPERTS * (CHUNK - 1) + CHUNK, V7X_LANES)
SORT_CHUNKS = SORT_ROWS // CHUNK
USED_CHUNKS_MAX = (TOP_K * TOK_TILE + N_EXPERTS * (CHUNK - 1)) // CHUNK


def _sigmoid(v):
    return 1.0 / (1.0 + jnp.exp(-v))


def _nt_dims():
    return (((1,), (1,)), ((), ()))


def _pack_halves(v, *, is_bf16_exact):
    c = v.shape[1] // 2
    bits = lax.bitcast_convert_type(v, U32)
    if not is_bf16_exact:
        bits = bits + jnp.uint32(0x7FFF) + ((bits >> 16) & jnp.uint32(1))
    return (bits[:, :c] >> 16) | (bits[:, c:] & jnp.uint32(0xFFFF0000))


def _unpack_halves(u):
    lo = lax.bitcast_convert_type(u << 16, F32).astype(BF16)
    hi = lax.bitcast_convert_type(u & jnp.uint32(0xFFFF0000), F32).astype(BF16)
    return jnp.concatenate([lo, hi], axis=1)


def _ada_kernel(c_ref, w_ref, b_ref, lq1_ref, lk1_ref, lq2_ref, lk2_ref, mod_ref, lam_ref, *, lam_init):
    c = c_ref[...]
    sc = c * _sigmoid(c)
    mod_ref[...] = jnp.dot(sc, w_ref[...], precision=HIGHEST, preferred_element_type=F32) + b_ref[...]
    s1 = jnp.sum(lq1_ref[...] * lk1_ref[...], axis=-1, keepdims=True)
    s2 = jnp.sum(lq2_ref[...] * lk2_ref[...], axis=-1, keepdims=True)
    lam = jnp.exp(s1) - jnp.exp(s2) + lam_init
    lam_ref[...] = jnp.broadcast_to(lam, lam_ref.shape)


def _ada(c, w_ada, b_ada, lq1, lk1, lq2, lk2, lam_init):
    B, D = c.shape
    cols = w_ada.shape[1]
    bc = D
    vec = pl.BlockSpec((1, ATTN_HEAD_DIM), lambda j: (0, 0))
    return pl.pallas_call(
        functools.partial(_ada_kernel, lam_init=lam_init),
        name="ada",
        grid=(cols // bc,),
        in_specs=[
            pl.BlockSpec((B, D), lambda j: (0, 0)),
            pl.BlockSpec((D, bc), lambda j: (0, j)),
            pl.BlockSpec((1, bc), lambda j: (0, j)),
            vec, vec, vec, vec,
        ],
        out_specs=[
            pl.BlockSpec((B, bc), lambda j: (0, j)),
            pl.BlockSpec((V7X_SUBLANES, V7X_LANES), lambda j: (0, 0)),
        ],
        out_shape=[
            jax.ShapeDtypeStruct((B, cols), F32),
            jax.ShapeDtypeStruct((V7X_SUBLANES, V7X_LANES), F32),
        ],
        compiler_params=pltpu.CompilerParams(dimension_semantics=("arbitrary",)),
    )(c, w_ada, b_ada, lq1, lk1, lq2, lk2)


def _group_rms_scale(t, lo):
    sq = t * t
    s1 = jnp.sum(jnp.where(lo, sq, 0.0), axis=-1, keepdims=True)
    s2 = jnp.sum(jnp.where(lo, 0.0, sq), axis=-1, keepdims=True)
    r1 = lax.rsqrt(s1 * (1.0 / ATTN_HEAD_DIM) + EPS)
    r2 = lax.rsqrt(s2 * (1.0 / ATTN_HEAD_DIM) + EPS)
    return jnp.where(lo, r1, r2)


def _inproj_kernel(x_ref, mod_ref, g1_ref, w_ref, qg_ref, kg_ref, hglu_ref, q_ref, k_ref, v_ref):
    x = x_ref[...]
    ms = jnp.mean(x * x, axis=-1, keepdims=True)
    shift = mod_ref[0:1, :]
    scale = mod_ref[1:2, :]
    h = ((x * lax.rsqrt(ms + EPS)) * g1_ref[...]) * (1.0 + scale) + shift
    hb = h.astype(BF16)
    c2 = 2 * CONV_CH
    ag = jnp.dot(hb, w_ref[:, 0:c2], preferred_element_type=F32)
    hglu_ref[...] = ag[:, :CONV_CH] * _sigmoid(ag[:, CONV_CH:])
    lo = lax.broadcasted_iota(I32, (1, V7X_LANES), 1) < ATTN_HEAD_DIM
    for src_col, g_ref, o_ref in ((c2, qg_ref, q_ref), (c2 + ATTN_WIDTH, kg_ref, k_ref)):
        t = jnp.dot(hb, w_ref[:, src_col:src_col + ATTN_WIDTH], preferred_element_type=F32)
        for hd in range(ATTN_HEADS):
            sl = slice(hd * V7X_LANES, (hd + 1) * V7X_LANES)
            th = t[:, sl]
            o_ref[:, sl] = (th * _group_rms_scale(th, lo) * g_ref[:, sl]).astype(BF16)
    v0 = c2 + 2 * ATTN_WIDTH
    v_ref[...] = jnp.dot(hb, w_ref[:, v0:v0 + ATTN_WIDTH], preferred_element_type=F32).astype(BF16)


def _inproj(x2d, mod3, g1, w_in_b, qg, kg, seq):
    N, D = x2d.shape
    tm = INPROJ_ROWS
    steps_per_seq = seq // tm
    cols = w_in_b.shape[1]
    row = lambda i: (i, 0)
    const = lambda i: (0, 0)
    return pl.pallas_call(
        _inproj_kernel,
        name="inproj",
        grid=(N // tm,),
        in_specs=[
            pl.BlockSpec((tm, D), row),
            pl.BlockSpec((None, N_MOD, D), lambda i: (i // steps_per_seq, 0, 0)),
            pl.BlockSpec((1, D), const),
            pl.BlockSpec((D, cols), const),
            pl.BlockSpec((1, ATTN_WIDTH), const),
            pl.BlockSpec((1, ATTN_WIDTH), const),
        ],
        out_specs=[
            pl.BlockSpec((tm, CONV_CH), row),
            pl.BlockSpec((tm, ATTN_WIDTH), row),
            pl.BlockSpec((tm, ATTN_WIDTH), row),
            pl.BlockSpec((tm, ATTN_WIDTH), row),
        ],
        out_shape=[
            jax.ShapeDtypeStruct((N, CONV_CH), F32),
            jax.ShapeDtypeStruct((N, ATTN_WIDTH), BF16),
            jax.ShapeDtypeStruct((N, ATTN_WIDTH), BF16),
            jax.ShapeDtypeStruct((N, ATTN_WIDTH), BF16),
        ],
        compiler_params=pltpu.CompilerParams(
            dimension_semantics=("parallel",), vmem_limit_bytes=48 * 1024 * 1024),
    )(x2d, mod3, g1, w_in_b, qg, kg)


def _attn_kernel(lam_ref, slope_ref, q_ref, k_ref, v_ref, sg_ref, o_ref,
                 raw_ref, s_ref, p_ref, mrun_ref, lrun_ref, acc_ref, *, tq, kb, lam_init):
    hd = pl.program_id(1)
    i = pl.program_id(2)
    lam = lam_ref[0]
    slope = slope_ref[hd]
    S = k_ref.shape[0]
    nkb = S // kb
    lanes = V7X_LANES
    lo = lax.broadcasted_iota(I32, (1, lanes), 1) < ATTN_HEAD_DIM
    q = q_ref[...]
    zero = jnp.zeros_like(q)
    q2 = jnp.concatenate([jnp.where(lo, q, zero), jnp.where(lo, zero, q)], axis=0)
    qpos = (i * tq + lax.broadcasted_iota(I32, (tq, 1), 0)).astype(F32) * slope

    def key_rows(j):
        if isinstance(j, int):
            return slice(j * kb, (j + 1) * kb)
        return pl.ds(pl.multiple_of(j * kb, kb), kb)

    def pipelined(produce, consume):
        produce(0, 0)
        pairs = nkb - 1

        for t in range(pairs // 2):
            j = 2 * t
            produce(j + 1, 1)
            consume(j, 0)
            produce(j + 2, 0)
            consume(j + 1, 1)

        for j in range(pairs // 2 * 2, pairs):
            produce(j + 1, (j + 1) % 2)
            consume(j, j % 2)
        consume(nkb - 1, (nkb - 1) % 2)

    def scores(j, slot):
        raw_ref[slot] = lax.dot_general(q2, k_ref[key_rows(j), :], _nt_dims(), preferred_element_type=F32)

    def bias_and_max(j, slot):
        kpos = (j * kb + lax.broadcasted_iota(I32, (1, kb), 1)).astype(F32) * slope
        bias = jnp.abs(kpos - qpos)
        s = raw_ref[slot] - jnp.concatenate([bias, bias], axis=0)
        s_ref[j] = s
        mx = s[:, 0:lanes]
        for c in range(1, kb // lanes):
            mx = jnp.maximum(mx, s[:, c * lanes:(c + 1) * lanes])
        mrun_ref[...] = jnp.maximum(mrun_ref[...], mx)

    mrun_ref[...] = jnp.full(mrun_ref.shape, -jnp.inf, F32)
    pipelined(scores, bias_and_max)
    m = jnp.max(mrun_ref[...], axis=-1, keepdims=True)

    def probs(j, slot):
        p = jnp.exp2(s_ref[j] - m)
        ps = p[:, 0:lanes]
        for c in range(1, kb // lanes):
            ps = ps + p[:, c * lanes:(c + 1) * lanes]
        lrun_ref[...] += ps
        p_ref[slot] = p.astype(BF16)

    def pv(j, slot):
        acc_ref[...] += jnp.dot(p_ref[slot], v_ref[key_rows(j), :], preferred_element_type=F32)

    lrun_ref[...] = jnp.zeros(lrun_ref.shape, F32)
    acc_ref[...] = jnp.zeros(acc_ref.shape, F32)
    pipelined(probs, pv)
    l = jnp.sum(lrun_ref[...], axis=-1, keepdims=True)
    acc = acc_ref[...]
    o = acc[:tq] * (1.0 / l[:tq]) - acc[tq:] * (lam / l[tq:])
    ms = jnp.mean(o * o, axis=-1, keepdims=True)
    o = (o * lax.rsqrt(ms + EPS)) * sg_ref[...] * (1.0 - lam_init)
    o_ref[...] = o.astype(BF16)


def _attention(lam1, slopes, q, k, v, sg, batch, seq, lam_init):
    tq = ATTN_Q_ROWS
    nq = seq // tq
    smem = pl.BlockSpec(memory_space=pltpu.SMEM)
    return pl.pallas_call(
        functools.partial(_attn_kernel, tq=tq, kb=ATTN_KEY_BLOCK, lam_init=lam_init),
        name="attn",
        grid=(batch, ATTN_HEADS, nq),
        in_specs=[
            smem, smem,
            pl.BlockSpec((tq, ATTN_V_DIM), lambda b, h, i: (b * nq + i, h)),
            pl.BlockSpec((seq, ATTN_V_DIM), lambda b, h, i: (b, h)),
            pl.BlockSpec((seq, ATTN_V_DIM), lambda b, h, i: (b, h)),
            pl.BlockSpec((1, ATTN_V_DIM), lambda b, h, i: (0, 0)),
        ],
        out_specs=pl.BlockSpec((tq, ATTN_V_DIM), lambda b, h, i: (b * nq + i, h)),
        out_shape=jax.ShapeDtypeStruct(q.shape, BF16),
        scratch_shapes=[
            pltpu.VMEM((2, 2 * tq, ATTN_KEY_BLOCK), F32),
            pltpu.VMEM((seq // ATTN_KEY_BLOCK, 2 * tq, ATTN_KEY_BLOCK), F32),
            pltpu.VMEM((2, 2 * tq, ATTN_KEY_BLOCK), BF16),
            pltpu.VMEM((2 * tq, V7X_LANES), F32),
            pltpu.VMEM((2 * tq, V7X_LANES), F32),
            pltpu.VMEM((2 * tq, ATTN_V_DIM), F32),
        ],
        compiler_params=pltpu.CompilerParams(
            dimension_semantics=("parallel", "parallel", "parallel"),
            vmem_limit_bytes=48 * 1024 * 1024),
    )(lam1, slopes, q, k, v, sg)


def _mixout_kernel(prev_ref, main_ref, next_ref, attn_ref, x_ref, mod_ref, cw_ref, cb_ref, lg_ref, lb_ref,
                   wo_ref, x1_ref, win_ref, sh_ref, *, tiles_per_seq):
    T = main_ref.shape[0]
    j = lax.rem(pl.program_id(0), tiles_per_seq)
    H = CONV_HALO
    win_ref[0:H, :] = jnp.where(j == 0, 0.0, prev_ref[...])
    win_ref[H:H + T, :] = main_ref[...]
    win_ref[H + T:H + T + H, :] = jnp.where(j == tiles_per_seq - 1, 0.0, next_ref[...])
    rows = sh_ref.shape[1]
    for r in range(1, V7X_SUBLANES):
        sh_ref[r] = win_ref[r:r + rows, :]
    acc = jnp.broadcast_to(cb_ref[...], (T, CONV_CH))
    off = H - CONV_KERNEL // 2
    for tap in range(CONV_KERNEL):
        a, r = divmod(off + tap, V7X_SUBLANES)
        src = win_ref if r == 0 else sh_ref.at[r]
        acc = acc + cw_ref[tap:tap + 1, :] * src[a * V7X_SUBLANES:a * V7X_SUBLANES + T, :]
    mu = jnp.mean(acc, axis=-1, keepdims=True)
    xc = acc - mu
    var = jnp.mean(xc * xc, axis=-1, keepdims=True)
    y = (xc * lax.rsqrt(var + EPS)) * lg_ref[...] + lb_ref[...]
    y = y * _sigmoid(y)
    mix = jnp.dot(y.astype(BF16), wo_ref[0:CONV_CH, :], preferred_element_type=F32)
    mix = mix + jnp.dot(attn_ref[...], wo_ref[CONV_CH:CONV_CH + ATTN_WIDTH, :], preferred_element_type=F32)
    x1_ref[...] = x_ref[...] + mod_ref[2:3, :] * mix


def _mixout(hglu, attn, x2d, mod3, cw, cb, lg, lb, wo_b, seq):
    N, D = x2d.shape
    T = TOK_TILE
    tiles_per_seq = seq // T
    hpt = T // CONV_HALO
    n_halo = N // CONV_HALO
    row = lambda i: (i, 0)
    const = lambda i: (0, 0)
    return pl.pallas_call(
        functools.partial(_mixout_kernel, tiles_per_seq=tiles_per_seq),
        name="mixout",
        grid=(N // T,),
        in_specs=[
            pl.BlockSpec((CONV_HALO, CONV_CH), lambda i: (jnp.maximum(i * hpt - 1, 0), 0)),
            pl.BlockSpec((T, CONV_CH), row),
            pl.BlockSpec((CONV_HALO, CONV_CH), lambda i: (jnp.minimum((i + 1) * hpt, n_halo - 1), 0)),
            pl.BlockSpec((T, ATTN_WIDTH), row),
            pl.BlockSpec((T, D), row),
            pl.BlockSpec((None, N_MOD, D), lambda i: (i // tiles_per_seq, 0, 0)),
            pl.BlockSpec(cw.shape, const),
            pl.BlockSpec((1, CONV_CH), const),
            pl.BlockSpec((1, CONV_CH), const),
            pl.BlockSpec((1, CONV_CH), const),
            pl.BlockSpec(wo_b.shape, const),
        ],
        out_specs=pl.BlockSpec((T, D), row),
        out_shape=jax.ShapeDtypeStruct((N, D), F32),
        scratch_shapes=[
            pltpu.VMEM((T + 2 * CONV_HALO, CONV_CH), F32),
            pltpu.VMEM((V7X_SUBLANES, T + 2 * CONV_HALO - V7X_SUBLANES, CONV_CH), F32),
        ],
        compiler_params=pltpu.CompilerParams(
            dimension_semantics=("parallel",), vmem_limit_bytes=48 * 1024 * 1024),
    )(hglu, hglu, hglu, attn, x2d, mod3, cw, cb, lg, lb, wo_b)


def _route_kernel(x1_ref, mod_ref, g2_ref, wr_ref, br_ref, xs_ref, rt_ref, ce_ref, *, n_tiles):
    i = pl.program_id(0)

    @pl.when(i < n_tiles)
    def _():
        _route_tile(x1_ref, mod_ref, g2_ref, wr_ref, br_ref, xs_ref, rt_ref, ce_ref)

    @pl.when(i >= n_tiles)
    def _():
        xs_ref[...] = jnp.zeros(xs_ref.shape, U32)
        rt_ref[...] = jnp.zeros(rt_ref.shape, F32)
        ce_ref[...] = jnp.full(ce_ref.shape, N_EXPERTS, I32)


def _route_tile(x1_ref, mod_ref, g2_ref, wr_ref, br_ref, xs_ref, rt_ref, ce_ref):
    T = x1_ref.shape[0]
    E = N_EXPERTS
    L = xs_ref.shape[0]
    x1 = x1_ref[...]
    ms = jnp.mean(x1 * x1, axis=-1, keepdims=True)
    h2 = ((x1 * lax.rsqrt(ms + EPS)) * g2_ref[...]) * (1.0 + mod_ref[4:5, :]) + mod_ref[3:4, :]
    logits = lax.dot_general(wr_ref[...], h2, _nt_dims(), precision=HIGHEST,
                             preferred_element_type=F32) + br_ref[...]
    e_iota = lax.broadcasted_iota(I32, (E, T), 0)
    sels, tops = [], []
    l = logits
    for _ in range(TOP_K):
        m = jnp.max(l, axis=0, keepdims=True)
        idx = jnp.min(jnp.where(l == m, e_iota, E), axis=0, keepdims=True)
        sel = e_iota == idx
        l = jnp.where(sel, -jnp.inf, l)
        sels.append(sel)
        tops.append(m)
    ws = [jnp.exp(m - tops[0]) for m in tops]
    den = ws[0] + ws[1] + ws[2] + ws[3]
    gates = [w / den for w in ws]
    multi = jnp.zeros((E, T), F32)
    for sel in sels:
        multi = multi + jnp.where(sel, 1.0, 0.0)
    r_iota = lax.broadcasted_iota(I32, (T, T + V7X_LANES), 0)
    c_iota = lax.broadcasted_iota(I32, (T, T + V7X_LANES), 1)
    tri = jnp.where((r_iota < c_iota) | (c_iota >= T), 1.0, 0.0).astype(BF16)
    rk = jnp.dot(multi.astype(BF16), tri, preferred_element_type=F32)
    rank = rk[:, :T]
    cnt = rk[:, T:T + V7X_LANES]
    n8 = jnp.floor((cnt + (CHUNK - 1)) * (1.0 / CHUNK))
    e_iota_l = lax.broadcasted_iota(I32, (E, V7X_LANES), 0)
    lo8 = jnp.zeros((E, V7X_LANES), F32)
    for e in range(E - 1):
        lo8 = lo8 + jnp.where(e_iota_l > e, n8[e:e + 1, :], 0.0)
    base = lo8[:, 0:1] * float(CHUNK) + rank
    dests = [jnp.sum(jnp.where(sel, base, 0.0), axis=0, keepdims=True) for sel in sels]
    for k in range(TOP_K):
        rt_ref[k:k + 1, :] = dests[k]
        rt_ref[TOP_K + k:TOP_K + k + 1, :] = gates[k]
    end8 = lo8[:, 0:1] + n8[:, 0:1]
    s_iota = lax.broadcasted_iota(I32, (E, ce_ref.shape[1]), 1).astype(F32)
    ce_ref[...] = jnp.sum(jnp.where(end8 <= s_iota, 1.0, 0.0), axis=0, keepdims=True).astype(I32)
    j_iota = lax.broadcasted_iota(I32, (L, T), 0).astype(F32)
    pm = jnp.zeros((L, T), F32)
    for d in dests:
        pm = pm + jnp.where(j_iota == d, 1.0, 0.0)
    xs_ref[...] = _pack_halves(jnp.dot(pm.astype(BF16), h2.astype(BF16), preferred_element_type=F32),
                               is_bf16_exact=True)


def _route(x1, mod3, g2, wr_t, br, seq, trash_tiles):
    N, D = x1.shape
    T = TOK_TILE
    tiles_per_seq = seq // T
    n_tiles = N // T
    n_ext = n_tiles + trash_tiles
    L = SORT_ROWS
    ce_w = _round_up(SORT_CHUNKS, V7X_LANES)
    const = lambda i: (0, 0)
    real = lambda i: jnp.minimum(i, n_tiles - 1)
    return pl.pallas_call(
        functools.partial(_route_kernel, n_tiles=n_tiles),
        name="route",
        grid=(n_ext,),
        in_specs=[
            pl.BlockSpec((T, D), lambda i: (real(i), 0)),
            pl.BlockSpec((None, N_MOD, D), lambda i: (real(i) // tiles_per_seq, 0, 0)),
            pl.BlockSpec((1, D), const),
            pl.BlockSpec(wr_t.shape, const),
            pl.BlockSpec(br.shape, const),
        ],
        out_specs=[
            pl.BlockSpec((L, D // 2), lambda i: (i, 0)),
            pl.BlockSpec((None, 2 * TOP_K, T), lambda i: (i, 0, 0)),
            pl.BlockSpec((None, 1, ce_w), lambda i: (i, 0, 0)),
        ],
        out_shape=[
            jax.ShapeDtypeStruct((n_ext * L, D // 2), U32),
            jax.ShapeDtypeStruct((n_ext, 2 * TOP_K, T), F32),
            jax.ShapeDtypeStruct((n_ext, 1, ce_w), I32),
        ],
        compiler_params=pltpu.CompilerParams(
            dimension_semantics=("parallel",), vmem_limit_bytes=48 * 1024 * 1024),
    )(x1, mod3, g2, wr_t, br)


def _expert_kernel(be_ref, nb_ref, slot_ref, xs_hbm, wgu_ref, bgu_ref, wd_ref, bd_ref, ys_hbm,
                   xbuf, obuf, wgu_b, wd_b, sem_in, sem_out):
    b = pl.program_id(0)
    nb = nb_ref[0]
    dff = wd_ref.shape[0]

    def gather_copy(blk, slot, j):
        chunk = slot_ref[blk * CHUNKS_PER_BLOCK + j]
        return pltpu.make_async_copy(xs_hbm.at[chunk], xbuf.at[slot, pl.ds(j * CHUNK, CHUNK)], sem_in.at[slot])

    def scatter_copy(blk, slot, j):
        chunk = slot_ref[blk * CHUNKS_PER_BLOCK + j]
        return pltpu.make_async_copy(obuf.at[slot, pl.ds(j * CHUNK, CHUNK)], ys_hbm.at[chunk], sem_out.at[slot])

    def start_gather(blk, slot):
        for j in range(CHUNKS_PER_BLOCK):
            gather_copy(blk, slot, j).start()

    def wait_gather(blk, slot):
        for j in range(CHUNKS_PER_BLOCK):
            gather_copy(blk, slot, j).wait()

    def start_scatter(blk, slot):
        for j in range(CHUNKS_PER_BLOCK):
            scatter_copy(blk, slot, j).start()

    def wait_scatter(blk, slot):
        for j in range(CHUNKS_PER_BLOCK):
            scatter_copy(blk, slot, j).wait()

    @pl.when(b < nb)
    def _():
        slot = lax.rem(b, 2)

        @pl.when(b == 0)
        def _():
            start_gather(0, 0)

        @pl.when(b + 1 < nb)
        def _():
            start_gather(b + 1, 1 - slot)

        @pl.when((b == 0) | (be_ref[b] != be_ref[jnp.maximum(b - 1, 0)]))
        def _():
            wgu_b[...] = wgu_ref[...].astype(BF16)
            wd_b[...] = wd_ref[...].astype(BF16)

        wait_gather(b, slot)

        @pl.when(b >= 2)
        def _():
            wait_scatter(b - 2, slot)

        xb = _unpack_halves(xbuf[slot])
        gu = jnp.dot(xb, wgu_b[...], preferred_element_type=F32) + bgu_ref[...]
        g = jnp.minimum(gu[:, :dff], SWIGLU_LIMIT)
        u = jnp.clip(gu[:, dff:], -SWIGLU_LIMIT, SWIGLU_LIMIT)
        y = (u + 1.0) * (g * _sigmoid(SWIGLU_ALPHA * g))
        out = jnp.dot(y.astype(BF16), wd_b[...], preferred_element_type=F32) + bd_ref[...]
        obuf[slot] = _pack_halves(out, is_bf16_exact=False)
        start_scatter(b, slot)

        @pl.when(b == nb - 1)
        def _():
            wait_scatter(b, slot)

            @pl.when(b >= 1)
            def _():
                wait_scatter(b - 1, 1 - slot)


def _experts(blk_expert, nb_total, slots, xs, wgu, bgu, wd, bd):
    nb_max = blk_expert.shape[0]
    half = xs.shape[1]
    D = 2 * half
    f2 = wgu.shape[2]
    dff = wd.shape[1]
    chunks = xs.reshape(xs.shape[0] // CHUNK, CHUNK, half)
    wmap = lambda b, be, nb, sl: (be[b], 0, 0)
    grid_spec = pltpu.PrefetchScalarGridSpec(
        num_scalar_prefetch=3,
        grid=(nb_max,),
        in_specs=[
            pl.BlockSpec(memory_space=pl.ANY),
            pl.BlockSpec((None, D, f2), wmap),
            pl.BlockSpec((None, 1, f2), wmap),
            pl.BlockSpec((None, dff, D), wmap),
            pl.BlockSpec((None, 1, D), wmap),
        ],
        out_specs=pl.BlockSpec(memory_space=pl.ANY),
        scratch_shapes=[
            pltpu.VMEM((2, BLOCK_ROWS, half), U32),
            pltpu.VMEM((2, BLOCK_ROWS, half), U32),
            pltpu.VMEM((D, f2), BF16),
            pltpu.VMEM((dff, D), BF16),
            pltpu.SemaphoreType.DMA((2,)),
            pltpu.SemaphoreType.DMA((2,)),
        ],
    )
    ys = pl.pallas_call(
        _expert_kernel,
        name="experts",
        grid_spec=grid_spec,
        out_shape=jax.ShapeDtypeStruct(chunks.shape, U32),
        input_output_aliases={3: 0},
        compiler_params=pltpu.CompilerParams(
            dimension_semantics=("arbitrary",), vmem_limit_bytes=56 * 1024 * 1024),
    )(blk_expert, nb_total, slots, chunks, wgu, bgu, wd, bd)
    return ys.reshape(xs.shape)


def _combine_kernel(ys_ref, rt_ref, x1_ref, mod_ref, x2_ref):
    T = x1_ref.shape[0]
    L = ys_ref.shape[0]
    j_iota = lax.broadcasted_iota(I32, (T, L), 1).astype(F32)
    wm = jnp.zeros((T, L), F32)
    for k in range(TOP_K):
        wm = wm + jnp.where(j_iota == rt_ref[:, k:k + 1], rt_ref[:, TOP_K + k:TOP_K + k + 1], 0.0)
    y = jnp.dot(wm.astype(BF16), _unpack_halves(ys_ref[...]), preferred_element_type=F32)
    x2_ref[...] = x1_ref[...] + mod_ref[5:6, :] * y


def _combine(ys, rt, x1, mod3, seq):
    N, D = x1.shape
    T = TOK_TILE
    tiles_per_seq = seq // T
    L = SORT_ROWS
    return pl.pallas_call(
        _combine_kernel,
        name="combine",
        grid=(N // T,),
        in_specs=[
            pl.BlockSpec((L, D // 2), lambda i: (i, 0)),
            pl.BlockSpec((T, 2 * TOP_K), lambda i: (i, 0)),
            pl.BlockSpec((T, D), lambda i: (i, 0)),
            pl.BlockSpec((None, N_MOD, D), lambda i: (i // tiles_per_seq, 0, 0)),
        ],
        out_specs=pl.BlockSpec((T, D), lambda i: (i, 0)),
        out_shape=jax.ShapeDtypeStruct((N, D), F32),
        compiler_params=pltpu.CompilerParams(
            dimension_semantics=("parallel",), vmem_limit_bytes=48 * 1024 * 1024),
    )(ys, rt, x1, mod3)


def _block_tables(chunk_expert, n_tiles):
    E, cpb = N_EXPERTS, CHUNKS_PER_BLOCK
    keys = chunk_expert[:, 0, :SORT_CHUNKS].reshape(-1)
    order = jnp.argsort(keys, stable=True).astype(I32)
    cc = jnp.sum((keys[:, None] == jnp.arange(E, dtype=I32)[None, :]).astype(I32), axis=0)
    cend = jnp.cumsum(cc)
    cstart = cend - cc
    nbk = (cc + cpb - 1) // cpb
    bend = jnp.cumsum(nbk)
    bstart = bend - nbk
    nb_total = bend[-1:]
    nb_max = -(-(n_tiles * USED_CHUNKS_MAX) // cpb) + E
    bidx = jnp.arange(nb_max, dtype=I32)
    be = jnp.minimum(jnp.sum((bidx[:, None] >= bend[None, :]).astype(I32), axis=1), E - 1)
    j = jnp.arange(cpb, dtype=I32)[None, :]
    pos = cstart[be][:, None] + (bidx - bstart[be])[:, None] * cpb + j
    valid = (pos < cend[be][:, None]) & (bidx < nb_total[0])[:, None]
    src = order[jnp.clip(pos, 0, order.shape[0] - 1)]
    trash = be[:, None] * cpb + j
    slots = jnp.where(valid, src, n_tiles * SORT_CHUNKS + trash).astype(I32)
    return be, nb_total.astype(I32), slots.reshape(-1)


def kernel(x, c, w_ada, b_ada, norm1_g, w_in, q_norm_g, k_norm_g, lambda_q1, lambda_k1, lambda_q2,
           lambda_k2, subln_g, conv_w, conv_b, conv_ln_g, conv_ln_b, w_out, norm2_g, w_router, b_router,
           w_gate_up, b_gate_up, w_down, b_down):
    B, S, D = x.shape
    N = B * S
    assert S % INPROJ_ROWS == 0 and S % ATTN_Q_ROWS == 0 and S % TOK_TILE == 0
    n_tiles = N // TOK_TILE
    trash_tiles = -(-(N_EXPERTS * CHUNKS_PER_BLOCK) // SORT_CHUNKS)
    log2e = math.log2(math.e)
    slopes = jnp.exp2(-8.0 * jnp.arange(1, ATTN_HEADS + 1, dtype=F32) / ATTN_HEADS) * log2e
    xt = x.reshape(N, D)
    for l in range(w_ada.shape[0]):
        lam_init = 0.8 - 0.6 * math.exp(-0.3 * l)
        row = lambda a: a[l][None, :]
        mod, lam = _ada(c, w_ada[l], row(b_ada), row(lambda_q1), row(lambda_k1), row(lambda_q2),
                        row(lambda_k2), lam_init)
        mod3 = mod.reshape(B, N_MOD, D)
        qg = jnp.tile(q_norm_g[l], 2 * ATTN_HEADS)[None, :] * (ATTN_HEAD_DIM ** -0.5 * log2e)
        kg = jnp.tile(k_norm_g[l], 2 * ATTN_HEADS)[None, :]
        hglu, q, k, v = _inproj(xt, mod3, row(norm1_g), w_in[l].astype(BF16), qg, kg, S)
        attn = _attention(lam[0, 0:1], slopes, q, k, v, row(subln_g), B, S, lam_init)
        cw = jnp.pad(conv_w[l], ((0, 1), (0, 0)))
        x1 = _mixout(hglu, attn, xt, mod3, cw, row(conv_b), row(conv_ln_g), row(conv_ln_b),
                     w_out[l].astype(BF16), S)
        xs, rt, ce = _route(x1, mod3, row(norm2_g), w_router[l].T, b_router[l][:, None], S, trash_tiles)
        be, nb_total, slots = _block_tables(ce[:n_tiles], n_tiles)
        ys = _experts(be, nb_total, slots, xs, w_gate_up[l], b_gate_up[l][:, None, :],
                      w_down[l], b_down[l][:, None, :])
        rt_tok = rt[:n_tiles].transpose(0, 2, 1).reshape(N, 2 * TOP_K)
        xt = _combine(ys, rt_tok, x1, mod3, S)
    return xt.reshape(B, S, D)
```

```python
import functools
import math

import jax
import jax.numpy as jnp
from jax import lax
from jax.experimental import pallas as pl
from jax.experimental.pallas import tpu as pltpu

F32 = jnp.float32
BF16 = jnp.bfloat16
I32 = jnp.int32
U32 = jnp.uint32
HIGHEST = lax.Precision.HIGHEST

CONV_CH = 512
CONV_KERNEL = 31
CONV_HALO = 16
ATTN_HEADS = 4
ATTN_HEAD_DIM = 64
ATTN_V_DIM = 2 * ATTN_HEAD_DIM
ATTN_WIDTH = ATTN_HEADS * ATTN_V_DIM
N_EXPERTS = 32
TOP_K = 4
SWIGLU_LIMIT = 7.0
SWIGLU_ALPHA = 1.702
N_MOD = 6
EPS = 1e-5

V7X_LANES = 128
V7X_SUBLANES = 8

INPROJ_ROWS = 512
ATTN_Q_ROWS = 1024
ATTN_SUB_ROWS = 256
TOK_TILE = 256
COMBINE_TILES_PER_STEP = 2
ROUTE_TILES_PER_STEP = 4
CHUNK = V7X_SUBLANES
BLOCK_ROWS = 512
CHUNKS_PER_BLOCK = BLOCK_ROWS // CHUNK


def _round_up(a, b):
    return (a + b - 1) // b * b


USED_CHUNKS_MAX = (TOP_K * TOK_TILE + N_EXPERTS * (CHUNK - 1)) // CHUNK
SORT_ROWS = _round_up(USED_CHUNKS_MAX * CHUNK, V7X_LANES)
SORT_CHUNKS = SORT_ROWS // CHUNK


def _sigmoid(v):
    return 1.0 / (1.0 + jnp.exp(-v))


def _nt_dims():
    return (((1,), (1,)), ((), ()))


def _pack_halves(v, *, is_bf16_exact):
    c = v.shape[1] // 2
    bits = lax.bitcast_convert_type(v, U32)
    if not is_bf16_exact:
        bits = bits + jnp.uint32(0x7FFF) + ((bits >> 16) & jnp.uint32(1))
    return (bits[:, :c] >> 16) | (bits[:, c:] & jnp.uint32(0xFFFF0000))


def _unpack_halves(u):
    lo = lax.bitcast_convert_type(u << 16, F32).astype(BF16)
    hi = lax.bitcast_convert_type(u & jnp.uint32(0xFFFF0000), F32).astype(BF16)
    return jnp.concatenate([lo, hi], axis=1)


def _ada_kernel(c_ref, w_ref, b_ref, lq1_ref, lk1_ref, lq2_ref, lk2_ref, mod_ref, lam_ref, *, lam_init):
    c = c_ref[...]
    sc = c * _sigmoid(c)
    mod_ref[...] = jnp.dot(sc, w_ref[...], precision=HIGHEST, preferred_element_type=F32) + b_ref[...]
    s1 = jnp.sum(lq1_ref[...] * lk1_ref[...], axis=-1, keepdims=True)
    s2 = jnp.sum(lq2_ref[...] * lk2_ref[...], axis=-1, keepdims=True)
    lam = jnp.exp(s1) - jnp.exp(s2) + lam_init
    lam_ref[...] = jnp.broadcast_to(lam, lam_ref.shape)


def _ada(c, w_ada, b_ada, lq1, lk1, lq2, lk2, lam_init):
    B, D = c.shape
    cols = w_ada.shape[1]
    bc = D
    vec = pl.BlockSpec((1, ATTN_HEAD_DIM), lambda j: (0, 0))
    return pl.pallas_call(
        functools.partial(_ada_kernel, lam_init=lam_init),
        name="ada",
        grid=(cols // bc,),
        in_specs=[
            pl.BlockSpec((B, D), lambda j: (0, 0)),
            pl.BlockSpec((D, bc), lambda j: (0, j)),
            pl.BlockSpec((1, bc), lambda j: (0, j)),
            vec, vec, vec, vec,
        ],
        out_specs=[
            pl.BlockSpec((B, bc), lambda j: (0, j)),
            pl.BlockSpec((V7X_SUBLANES, V7X_LANES), lambda j: (0, 0)),
        ],
        out_shape=[
            jax.ShapeDtypeStruct((B, cols), F32),
            jax.ShapeDtypeStruct((V7X_SUBLANES, V7X_LANES), F32),
        ],
        compiler_params=pltpu.CompilerParams(dimension_semantics=("arbitrary",)),
    )(c, w_ada, b_ada, lq1, lk1, lq2, lk2)


def _group_rms_scale(t, lo):
    sq = t * t
    s1 = jnp.sum(jnp.where(lo, sq, 0.0), axis=-1, keepdims=True)
    s2 = jnp.sum(jnp.where(lo, 0.0, sq), axis=-1, keepdims=True)
    r1 = lax.rsqrt(s1 * (1.0 / ATTN_HEAD_DIM) + EPS)
    r2 = lax.rsqrt(s2 * (1.0 / ATTN_HEAD_DIM) + EPS)
    return jnp.where(lo, r1, r2)


def _inproj_kernel(x_ref, mod_ref, g1_ref, w_ref, qg_ref, kg_ref, hglu_ref, q_ref, k_ref, v_ref):
    x = x_ref[...]
    ms = jnp.mean(x * x, axis=-1, keepdims=True)
    shift = mod_ref[0:1, :]
    scale = mod_ref[1:2, :]
    h = ((x * lax.rsqrt(ms + EPS)) * g1_ref[...]) * (1.0 + scale) + shift
    hb = h.astype(BF16)
    c2 = 2 * CONV_CH
    ag = jnp.dot(hb, w_ref[:, 0:c2], preferred_element_type=F32)
    hglu_ref[...] = ag[:, :CONV_CH] * _sigmoid(ag[:, CONV_CH:])
    lo = lax.broadcasted_iota(I32, (1, V7X_LANES), 1) < ATTN_HEAD_DIM
    for src_col, g_ref, o_ref in ((c2, qg_ref, q_ref), (c2 + ATTN_WIDTH, kg_ref, k_ref)):
        t = jnp.dot(hb, w_ref[:, src_col:src_col + ATTN_WIDTH], preferred_element_type=F32)
        for hd in range(ATTN_HEADS):
            sl = slice(hd * V7X_LANES, (hd + 1) * V7X_LANES)
            th = t[:, sl]
            o_ref[:, sl] = (th * _group_rms_scale(th, lo) * g_ref[:, sl]).astype(BF16)
    v0 = c2 + 2 * ATTN_WIDTH
    v_ref[...] = jnp.dot(hb, w_ref[:, v0:v0 + ATTN_WIDTH], preferred_element_type=F32).astype(BF16)


def _inproj(x2d, mod3, g1, w_in_b, qg, kg, seq):
    N, D = x2d.shape
    tm = INPROJ_ROWS
    steps_per_seq = seq // tm
    cols = w_in_b.shape[1]
    row = lambda i: (i, 0)
    const = lambda i: (0, 0)
    return pl.pallas_call(
        _inproj_kernel,
        name="inproj",
        grid=(N // tm,),
        in_specs=[
            pl.BlockSpec((tm, D), row),
            pl.BlockSpec((None, N_MOD, D), lambda i: (i // steps_per_seq, 0, 0)),
            pl.BlockSpec((1, D), const),
            pl.BlockSpec((D, cols), const),
            pl.BlockSpec((1, ATTN_WIDTH), const),
            pl.BlockSpec((1, ATTN_WIDTH), const),
        ],
        out_specs=[
            pl.BlockSpec((tm, CONV_CH), row),
            pl.BlockSpec((tm, ATTN_WIDTH), row),
            pl.BlockSpec((tm, ATTN_WIDTH), row),
            pl.BlockSpec((tm, ATTN_WIDTH), row),
        ],
        out_shape=[
            jax.ShapeDtypeStruct((N, CONV_CH), F32),
            jax.ShapeDtypeStruct((N, ATTN_WIDTH), BF16),
            jax.ShapeDtypeStruct((N, ATTN_WIDTH), BF16),
            jax.ShapeDtypeStruct((N, ATTN_WIDTH), BF16),
        ],
        compiler_params=pltpu.CompilerParams(
            dimension_semantics=("parallel",), vmem_limit_bytes=48 * 1024 * 1024),
    )(x2d, mod3, g1, w_in_b, qg, kg)


def _attn_kernel(lam_ref, slope_ref, q_ref, k_ref, v_ref, sg_ref, o_ref, *, tq, ts, lam_init):
    hd = pl.program_id(1)
    i = pl.program_id(2)
    lam = lam_ref[0]
    slope = slope_ref[hd]
    k = k_ref[...]
    v = v_ref[...]
    S = k.shape[0]
    lo = lax.broadcasted_iota(I32, (1, V7X_LANES), 1) < ATTN_HEAD_DIM
    kpos = lax.broadcasted_iota(I32, (1, S), 1).astype(F32) * slope

    def softmax_parts(qc, bias):
        s = lax.dot_general(qc, k, _nt_dims(), preferred_element_type=F32) - bias
        m = jnp.max(s, axis=-1, keepdims=True)
        p = jnp.exp2(s - m)
        return p, jnp.sum(p, axis=-1, keepdims=True)

    for sub in range(tq // ts):
        rows = slice(sub * ts, (sub + 1) * ts)
        q = q_ref[rows, :]
        zero = jnp.zeros_like(q)
        qpos = (i * tq + sub * ts + lax.broadcasted_iota(I32, (ts, 1), 0)).astype(F32) * slope
        bias = jnp.abs(kpos - qpos)
        p1, l1 = softmax_parts(jnp.where(lo, q, zero), bias)
        p2, l2 = softmax_parts(jnp.where(lo, zero, q), bias)
        p = p1 - p2 * (lam * l1 / l2)
        o = jnp.dot(p.astype(BF16), v, preferred_element_type=F32) * (1.0 / l1)
        ms = jnp.mean(o * o, axis=-1, keepdims=True)
        o = (o * lax.rsqrt(ms + EPS)) * sg_ref[...] * (1.0 - lam_init)
        o_ref[rows, :] = o.astype(BF16)


def _attention(lam1, slopes, q, k, v, sg, batch, seq, lam_init):
    tq = ATTN_Q_ROWS
    nq = seq // tq
    smem = pl.BlockSpec(memory_space=pltpu.SMEM)
    return pl.pallas_call(
        functools.partial(_attn_kernel, tq=tq, ts=ATTN_SUB_ROWS, lam_init=lam_init),
        name="attn",
        grid=(batch, ATTN_HEADS, nq),
        in_specs=[
            smem, smem,
            pl.BlockSpec((tq, ATTN_V_DIM), lambda b, h, i: (b * nq + i, h)),
            pl.BlockSpec((seq, ATTN_V_DIM), lambda b, h, i: (b, h)),
            pl.BlockSpec((seq, ATTN_V_DIM), lambda b, h, i: (b, h)),
            pl.BlockSpec((1, ATTN_V_DIM), lambda b, h, i: (0, 0)),
        ],
        out_specs=pl.BlockSpec((tq, ATTN_V_DIM), lambda b, h, i: (b * nq + i, h)),
        out_shape=jax.ShapeDtypeStruct(q.shape, BF16),
        compiler_params=pltpu.CompilerParams(
            dimension_semantics=("parallel", "parallel", "parallel"),
            vmem_limit_bytes=48 * 1024 * 1024),
    )(lam1, slopes, q, k, v, sg)


def _mixout_kernel(prev_ref, main_ref, next_ref, attn_ref, x_ref, mod_ref, cw_ref, cb_ref, lg_ref, lb_ref,
                   wo_ref, x1_ref, win_ref, sh_ref, *, tiles_per_seq):
    T = main_ref.shape[0]
    j = lax.rem(pl.program_id(0), tiles_per_seq)
    H = CONV_HALO
    win_ref[0:H, :] = jnp.where(j == 0, 0.0, prev_ref[...])
    win_ref[H:H + T, :] = main_ref[...]
    win_ref[H + T:H + T + H, :] = jnp.where(j == tiles_per_seq - 1, 0.0, next_ref[...])
    rows = sh_ref.shape[1]
    for r in range(1, V7X_SUBLANES):
        sh_ref[r] = win_ref[r:r + rows, :]
    acc = jnp.broadcast_to(cb_ref[...], (T, CONV_CH))
    off = H - CONV_KERNEL // 2
    for tap in range(CONV_KERNEL):
        a, r = divmod(off + tap, V7X_SUBLANES)
        src = win_ref if r == 0 else sh_ref.at[r]
        acc = acc + cw_ref[tap:tap + 1, :] * src[a * V7X_SUBLANES:a * V7X_SUBLANES + T, :]
    mu = jnp.mean(acc, axis=-1, keepdims=True)
    xc = acc - mu
    var = jnp.mean(xc * xc, axis=-1, keepdims=True)
    y = (xc * lax.rsqrt(var + EPS)) * lg_ref[...] + lb_ref[...]
    y = y * _sigmoid(y)
    mix = jnp.dot(y.astype(BF16), wo_ref[0:CONV_CH, :], preferred_element_type=F32)
    mix = mix + jnp.dot(attn_ref[...], wo_ref[CONV_CH:CONV_CH + ATTN_WIDTH, :], preferred_element_type=F32)
    x1_ref[...] = x_ref[...] + mod_ref[2:3, :] * mix


def _mixout(hglu, attn, x2d, mod3, cw, cb, lg, lb, wo_b, seq):
    N, D = x2d.shape
    T = TOK_TILE
    tiles_per_seq = seq // T
    hpt = T // CONV_HALO
    n_halo = N // CONV_HALO
    row = lambda i: (i, 0)
    const = lambda i: (0, 0)
    return pl.pallas_call(
        functools.partial(_mixout_kernel, tiles_per_seq=tiles_per_seq),
        name="mixout",
        grid=(N // T,),
        in_specs=[
            pl.BlockSpec((CONV_HALO, CONV_CH), lambda i: (jnp.maximum(i * hpt - 1, 0), 0)),
            pl.BlockSpec((T, CONV_CH), row),
            pl.BlockSpec((CONV_HALO, CONV_CH), lambda i: (jnp.minimum((i + 1) * hpt, n_halo - 1), 0)),
            pl.BlockSpec((T, ATTN_WIDTH), row),
            pl.BlockSpec((T, D), row),
            pl.BlockSpec((None, N_MOD, D), lambda i: (i // tiles_per_seq, 0, 0)),
            pl.BlockSpec(cw.shape, const),
            pl.BlockSpec((1, CONV_CH), const),
            pl.BlockSpec((1, CONV_CH), const),
            pl.BlockSpec((1, CONV_CH), const),
            pl.BlockSpec(wo_b.shape, const),
        ],
        out_specs=pl.BlockSpec((T, D), row),
        out_shape=jax.ShapeDtypeStruct((N, D), F32),
        scratch_shapes=[
            pltpu.VMEM((T + 2 * CONV_HALO, CONV_CH), F32),
            pltpu.VMEM((V7X_SUBLANES, T + 2 * CONV_HALO - V7X_SUBLANES, CONV_CH), F32),
        ],
        compiler_params=pltpu.CompilerParams(
            dimension_semantics=("parallel",), vmem_limit_bytes=48 * 1024 * 1024),
    )(hglu, hglu, hglu, attn, x2d, mod3, cw, cb, lg, lb, wo_b)


def _route_kernel(x1_ref, mod_ref, g2_ref, wr_ref, br_ref, xs_ref, rt_ref, ce_ref, *, n_steps):
    i = pl.program_id(0)

    @pl.when(i < n_steps)
    def _():
        _route_tiles(x1_ref, mod_ref, g2_ref, wr_ref, br_ref, xs_ref, rt_ref, ce_ref)

    @pl.when(i >= n_steps)
    def _():
        xs_ref[...] = jnp.zeros(xs_ref.shape, U32)
        rt_ref[...] = jnp.zeros(rt_ref.shape, F32)
        ce_ref[...] = jnp.full(ce_ref.shape, N_EXPERTS, I32)


def _route_tiles(x1_ref, mod_ref, g2_ref, wr_ref, br_ref, xs_ref, rt_ref, ce_ref):
    G, T, E, L = ROUTE_TILES_PER_STEP, TOK_TILE, N_EXPERTS, SORT_ROWS
    W = G * T
    x1 = x1_ref[...]
    ms = jnp.mean(x1 * x1, axis=-1, keepdims=True)
    h2 = ((x1 * lax.rsqrt(ms + EPS)) * g2_ref[...]) * (1.0 + mod_ref[4:5, :]) + mod_ref[3:4, :]
    h2b = h2.astype(BF16)
    logits = lax.dot_general(wr_ref[...], h2, _nt_dims(), precision=HIGHEST,
                             preferred_element_type=F32) + br_ref[...]
    e_iota = lax.broadcasted_iota(I32, (E, W), 0)
    sels, tops = [], []
    l = logits
    for _ in range(TOP_K):
        m = jnp.max(l, axis=0, keepdims=True)
        idx = jnp.min(jnp.where(l == m, e_iota, E), axis=0, keepdims=True)
        sel = e_iota == idx
        l = jnp.where(sel, -jnp.inf, l)
        sels.append(sel)
        tops.append(m)
    ws = [jnp.exp(m - tops[0]) for m in tops]
    den = ws[0] + ws[1] + ws[2] + ws[3]
    gates = [w / den for w in ws]
    multi = jnp.zeros((E, W), F32)
    for sel in sels:
        multi = multi + jnp.where(sel, 1.0, 0.0)
    multi_b = multi.astype(BF16)
    r_iota = lax.broadcasted_iota(I32, (T, T + V7X_LANES), 0)
    c_iota = lax.broadcasted_iota(I32, (T, T + V7X_LANES), 1)
    tri = jnp.where((r_iota < c_iota) | (c_iota >= T), 1.0, 0.0).astype(BF16)
    rks = [jnp.dot(multi_b[:, h * T:(h + 1) * T], tri, preferred_element_type=F32) for h in range(G)]
    cnt = jnp.concatenate([rk[:, T:T + V7X_LANES] for rk in rks], axis=1)
    n8 = jnp.floor((cnt + (CHUNK - 1)) * (1.0 / CHUNK))
    e_iota_l = lax.broadcasted_iota(I32, n8.shape, 0)
    lo8 = jnp.zeros(n8.shape, F32)
    for e in range(E - 1):
        lo8 = lo8 + jnp.where(e_iota_l > e, n8[e:e + 1, :], 0.0)
    s_iota = lax.broadcasted_iota(I32, (E, ce_ref.shape[2]), 1).astype(F32)
    j_iota = lax.broadcasted_iota(I32, (L, T), 0).astype(F32)
    for h in range(G):
        cols = slice(h * T, (h + 1) * T)
        lo_h = lo8[:, h * V7X_LANES:h * V7X_LANES + 1]
        n_h = n8[:, h * V7X_LANES:h * V7X_LANES + 1]
        base = lo_h * float(CHUNK) + rks[h][:, :T]
        dests = [jnp.sum(jnp.where(sel[:, cols], base, 0.0), axis=0, keepdims=True) for sel in sels]
        for k in range(TOP_K):
            rt_ref[h, k:k + 1, :] = dests[k]
            rt_ref[h, TOP_K + k:TOP_K + k + 1, :] = gates[k][:, cols]
        ce_ref[h] = jnp.sum(jnp.where(lo_h + n_h <= s_iota, 1.0, 0.0), axis=0, keepdims=True).astype(I32)
        pm = jnp.zeros((L, T), F32)
        for d in dests:
            pm = pm + jnp.where(j_iota == d, 1.0, 0.0)
        xs_ref[h * L:(h + 1) * L, :] = _pack_halves(
            jnp.dot(pm.astype(BF16), h2b[h * T:(h + 1) * T, :], preferred_element_type=F32), is_bf16_exact=True)


def _route(x1, mod3, g2, wr_t, br, seq, trash_tiles):
    N, D = x1.shape
    T = TOK_TILE
    tiles_per_seq = seq // T
    n_tiles = N // T
    n_ext = n_tiles + trash_tiles
    L = SORT_ROWS
    G = ROUTE_TILES_PER_STEP
    assert tiles_per_seq % G == 0 and n_ext % G == 0
    n_steps = n_tiles // G
    ce_w = _round_up(SORT_CHUNKS, V7X_LANES)
    const = lambda i: (0, 0)
    real = lambda i: jnp.minimum(i, n_steps - 1)
    return pl.pallas_call(
        functools.partial(_route_kernel, n_steps=n_steps),
        name="route",
        grid=(n_ext // G,),
        in_specs=[
            pl.BlockSpec((G * T, D), lambda i: (real(i), 0)),
            pl.BlockSpec((None, N_MOD, D), lambda i: (real(i) // (tiles_per_seq // G), 0, 0)),
            pl.BlockSpec((1, D), const),
            pl.BlockSpec(wr_t.shape, const),
            pl.BlockSpec(br.shape, const),
        ],
        out_specs=[
            pl.BlockSpec((G * L, D // 2), lambda i: (i, 0)),
            pl.BlockSpec((G, 2 * TOP_K, T), lambda i: (i, 0, 0)),
            pl.BlockSpec((G, 1, ce_w), lambda i: (i, 0, 0)),
        ],
        out_shape=[
            jax.ShapeDtypeStruct((n_ext * L, D // 2), U32),
            jax.ShapeDtypeStruct((n_ext, 2 * TOP_K, T), F32),
            jax.ShapeDtypeStruct((n_ext, 1, ce_w), I32),
        ],
        compiler_params=pltpu.CompilerParams(
            dimension_semantics=("parallel",), vmem_limit_bytes=48 * 1024 * 1024),
    )(x1, mod3, g2, wr_t, br)


def _expert_kernel(be_ref, nb_ref, slot_ref, xs_hbm, wgu_ref, bgu_ref, wd_ref, bd_ref, ys_hbm,
                   xbuf, obuf, wgu_b, wd_b, sem_in, sem_out):
    b = pl.program_id(0)
    nb = nb_ref[0]
    dff = wd_ref.shape[0]

    def gather_copy(blk, slot, j):
        chunk = slot_ref[blk * CHUNKS_PER_BLOCK + j]
        return pltpu.make_async_copy(xs_hbm.at[chunk], xbuf.at[slot, pl.ds(j * CHUNK, CHUNK)], sem_in.at[slot])

    def scatter_copy(blk, slot, j):
        chunk = slot_ref[blk * CHUNKS_PER_BLOCK + j]
        return pltpu.make_async_copy(obuf.at[slot, pl.ds(j * CHUNK, CHUNK)], ys_hbm.at[chunk], sem_out.at[slot])

    def start_gather(blk, slot):
        for j in range(CHUNKS_PER_BLOCK):
            gather_copy(blk, slot, j).start()

    def wait_gather(blk, slot):
        for j in range(CHUNKS_PER_BLOCK):
            gather_copy(blk, slot, j).wait()

    def start_scatter(blk, slot):
        for j in range(CHUNKS_PER_BLOCK):
            scatter_copy(blk, slot, j).start()

    def wait_scatter(blk, slot):
        for j in range(CHUNKS_PER_BLOCK):
            scatter_copy(blk, slot, j).wait()

    @pl.when(b < nb)
    def _():
        slot = lax.rem(b, 2)

        @pl.when(b == 0)
        def _():
            start_gather(0, 0)

        @pl.when(b + 1 < nb)
        def _():
            start_gather(b + 1, 1 - slot)

        @pl.when((b == 0) | (be_ref[b] != be_ref[jnp.maximum(b - 1, 0)]))
        def _():
            wgu_b[...] = wgu_ref[...].astype(BF16)
            wd_b[...] = wd_ref[...].astype(BF16)

        wait_gather(b, slot)

        @pl.when(b >= 2)
        def _():
            wait_scatter(b - 2, slot)

        xb = _unpack_halves(xbuf[slot])
        gu = jnp.dot(xb, wgu_b[...], preferred_element_type=F32) + bgu_ref[...]
        g = jnp.minimum(gu[:, :dff], SWIGLU_LIMIT)
        u = jnp.clip(gu[:, dff:], -SWIGLU_LIMIT, SWIGLU_LIMIT)
        y = (u + 1.0) * (g * _sigmoid(SWIGLU_ALPHA * g))
        out = jnp.dot(y.astype(BF16), wd_b[...], preferred_element_type=F32) + bd_ref[...]
        obuf[slot] = _pack_halves(out, is_bf16_exact=False)
        start_scatter(b, slot)

        @pl.when(b == nb - 1)
        def _():
            wait_scatter(b, slot)

            @pl.when(b >= 1)
            def _():
                wait_scatter(b - 1, 1 - slot)


def _experts(blk_expert, nb_total, slots, xs, wgu, bgu, wd, bd):
    nb_max = blk_expert.shape[0]
    half = xs.shape[1]
    D = 2 * half
    f2 = wgu.shape[2]
    dff = wd.shape[1]
    chunks = xs.reshape(xs.shape[0] // CHUNK, CHUNK, half)
    wmap = lambda b, be, nb, sl: (be[b], 0, 0)
    grid_spec = pltpu.PrefetchScalarGridSpec(
        num_scalar_prefetch=3,
        grid=(nb_max,),
        in_specs=[
            pl.BlockSpec(memory_space=pl.ANY),
            pl.BlockSpec((None, D, f2), wmap),
            pl.BlockSpec((None, 1, f2), wmap),
            pl.BlockSpec((None, dff, D), wmap),
            pl.BlockSpec((None, 1, D), wmap),
        ],
        out_specs=pl.BlockSpec(memory_space=pl.ANY),
        scratch_shapes=[
            pltpu.VMEM((2, BLOCK_ROWS, half), U32),
            pltpu.VMEM((2, BLOCK_ROWS, half), U32),
            pltpu.VMEM((D, f2), BF16),
            pltpu.VMEM((dff, D), BF16),
            pltpu.SemaphoreType.DMA((2,)),
            pltpu.SemaphoreType.DMA((2,)),
        ],
    )
    ys = pl.pallas_call(
        _expert_kernel,
        name="experts",
        grid_spec=grid_spec,
        out_shape=jax.ShapeDtypeStruct(chunks.shape, U32),
        input_output_aliases={3: 0},
        compiler_params=pltpu.CompilerParams(
            dimension_semantics=("arbitrary",), vmem_limit_bytes=56 * 1024 * 1024),
    )(blk_expert, nb_total, slots, chunks, wgu, bgu, wd, bd)
    return ys.reshape(xs.shape)


def _combine_kernel(ys_ref, rt_ref, x1_ref, mod_ref, x2_ref):
    T, L = TOK_TILE, SORT_ROWS
    j_iota = lax.broadcasted_iota(I32, (T, L), 1).astype(F32)
    for h in range(COMBINE_TILES_PER_STEP):
        rows = slice(h * T, (h + 1) * T)
        wm = jnp.zeros((T, L), F32)
        for k in range(TOP_K):
            wm = wm + jnp.where(j_iota == rt_ref[rows, k:k + 1], rt_ref[rows, TOP_K + k:TOP_K + k + 1], 0.0)
        y = jnp.dot(wm.astype(BF16), _unpack_halves(ys_ref[h * L:(h + 1) * L, :]), preferred_element_type=F32)
        x2_ref[rows, :] = x1_ref[rows, :] + mod_ref[5:6, :] * y


def _combine(ys, rt, x1, mod3, seq):
    N, D = x1.shape
    G = COMBINE_TILES_PER_STEP
    T = G * TOK_TILE
    tiles_per_seq = seq // T
    L = G * SORT_ROWS
    assert seq % T == 0
    return pl.pallas_call(
        _combine_kernel,
        name="combine",
        grid=(N // T,),
        in_specs=[
            pl.BlockSpec((L, D // 2), lambda i: (i, 0)),
            pl.BlockSpec((T, 2 * TOP_K), lambda i: (i, 0)),
            pl.BlockSpec((T, D), lambda i: (i, 0)),
            pl.BlockSpec((None, N_MOD, D), lambda i: (i // tiles_per_seq, 0, 0)),
        ],
        out_specs=pl.BlockSpec((T, D), lambda i: (i, 0)),
        out_shape=jax.ShapeDtypeStruct((N, D), F32),
        compiler_params=pltpu.CompilerParams(
            dimension_semantics=("parallel",), vmem_limit_bytes=48 * 1024 * 1024),
    )(ys, rt, x1, mod3)


def _block_tables(chunk_expert, n_tiles):
    E, cpb = N_EXPERTS, CHUNKS_PER_BLOCK
    keys = chunk_expert[:, 0, :SORT_CHUNKS].reshape(-1)
    order = jnp.argsort(keys, stable=True).astype(I32)
    cc = jnp.sum((keys[:, None] == jnp.arange(E, dtype=I32)[None, :]).astype(I32), axis=0)
    cend = jnp.cumsum(cc)
    cstart = cend - cc
    nbk = (cc + cpb - 1) // cpb
    bend = jnp.cumsum(nbk)
    bstart = bend - nbk
    nb_total = bend[-1:]
    nb_max = -(-(n_tiles * USED_CHUNKS_MAX) // cpb) + E
    bidx = jnp.arange(nb_max, dtype=I32)
    be = jnp.minimum(jnp.sum((bidx[:, None] >= bend[None, :]).astype(I32), axis=1), E - 1)
    j = jnp.arange(cpb, dtype=I32)[None, :]
    pos = cstart[be][:, None] + (bidx - bstart[be])[:, None] * cpb + j
    valid = (pos < cend[be][:, None]) & (bidx < nb_total[0])[:, None]
    src = order[jnp.clip(pos, 0, order.shape[0] - 1)]
    trash = be[:, None] * cpb + j
    slots = jnp.where(valid, src, n_tiles * SORT_CHUNKS + trash).astype(I32)
    return be, nb_total.astype(I32), slots.reshape(-1)


def kernel(x, c, w_ada, b_ada, norm1_g, w_in, q_norm_g, k_norm_g, lambda_q1, lambda_k1, lambda_q2,
           lambda_k2, subln_g, conv_w, conv_b, conv_ln_g, conv_ln_b, w_out, norm2_g, w_router, b_router,
           w_gate_up, b_gate_up, w_down, b_down):
    B, S, D = x.shape
    N = B * S
    assert S % INPROJ_ROWS == 0 and S % ATTN_Q_ROWS == 0 and S % TOK_TILE == 0
    n_tiles = N // TOK_TILE
    trash_tiles = _round_up(-(-(N_EXPERTS * CHUNKS_PER_BLOCK) // SORT_CHUNKS), ROUTE_TILES_PER_STEP)
    log2e = math.log2(math.e)
    slopes = jnp.exp2(-8.0 * jnp.arange(1, ATTN_HEADS + 1, dtype=F32) / ATTN_HEADS) * log2e
    xt = x.reshape(N, D)
    for l in range(w_ada.shape[0]):
        lam_init = 0.8 - 0.6 * math.exp(-0.3 * l)
        row = lambda a: a[l][None, :]
        mod, lam = _ada(c, w_ada[l], row(b_ada), row(lambda_q1), row(lambda_k1), row(lambda_q2),
                        row(lambda_k2), lam_init)
        mod3 = mod.reshape(B, N_MOD, D)
        qg = jnp.tile(q_norm_g[l], 2 * ATTN_HEADS)[None, :] * (ATTN_HEAD_DIM ** -0.5 * log2e)
        kg = jnp.tile(k_norm_g[l], 2 * ATTN_HEADS)[None, :]
        hglu, q, k, v = _inproj(xt, mod3, row(norm1_g), w_in[l].astype(BF16), qg, kg, S)
        attn = _attention(lam[0, 0:1], slopes, q, k, v, row(subln_g), B, S, lam_init)
        cw = jnp.pad(conv_w[l], ((0, 1), (0, 0)))
        x1 = _mixout(hglu, attn, xt, mod3, cw, row(conv_b), row(conv_ln_g), row(conv_ln_b),
                     w_out[l].astype(BF16), S)
        xs, rt, ce = _route(x1, mod3, row(norm2_g), w_router[l].T, b_router[l][:, None], S, trash_tiles)
        be, nb_total, slots = _block_tables(ce[:n_tiles], n_tiles)
        ys = _experts(be, nb_total, slots, xs, w_gate_up[l], b_gate_up[l][:, None, :],
                      w_down[l], b_down[l][:, None, :])
        rt_tok = rt[:n_tiles].transpose(0, 2, 1).reshape(N, 2 * TOP_K)
        xt = _combine(ys, rt_tok, x1, mod3, S)
    return xt.reshape(B, S, D)
```

```python
import functools
import math

import jax
import jax.numpy as jnp
from jax import lax
from jax.experimental import pallas as pl
from jax.experimental.pallas import tpu as pltpu

F32 = jnp.float32
BF16 = jnp.bfloat16
I32 = jnp.int32
U32 = jnp.uint32
HIGHEST = lax.Precision.HIGHEST

CONV_CH = 512
CONV_KERNEL = 31
CONV_HALO = 16
ATTN_HEADS = 4
ATTN_HEAD_DIM = 64
ATTN_V_DIM = 2 * ATTN_HEAD_DIM
ATTN_WIDTH = ATTN_HEADS * ATTN_V_DIM
N_EXPERTS = 32
TOP_K = 4
SWIGLU_LIMIT = 7.0
SWIGLU_ALPHA = 1.702
N_MOD = 6
EPS = 1e-5

V7X_LANES = 128
V7X_SUBLANES = 8

INPROJ_ROWS = 1024
ATTN_Q_ROWS = 1024
ATTN_SUB_ROWS = 256
TOK_TILE = 256
COMBINE_TILES_PER_STEP = 2
ROUTE_TILES_PER_STEP = 4
CHUNK = V7X_SUBLANES
BLOCK_ROWS = 512
CHUNKS_PER_BLOCK = BLOCK_ROWS // CHUNK


def _round_up(a, b):
    return (a + b - 1) // b * b


USED_CHUNKS_MAX = (TOP_K * TOK_TILE + N_EXPERTS * (CHUNK - 1)) // CHUNK
SORT_ROWS = _round_up(USED_CHUNKS_MAX * CHUNK, V7X_LANES)
SORT_CHUNKS = SORT_ROWS // CHUNK


def _sigmoid(v):
    return 1.0 / (1.0 + jnp.exp(-v))


def _nt_dims():
    return (((1,), (1,)), ((), ()))


def _pack_halves(v, *, is_bf16_exact):
    c = v.shape[1] // 2
    bits = lax.bitcast_convert_type(v, U32)
    if not is_bf16_exact:
        bits = bits + jnp.uint32(0x7FFF) + ((bits >> 16) & jnp.uint32(1))
    return (bits[:, :c] >> 16) | (bits[:, c:] & jnp.uint32(0xFFFF0000))


def _unpack_halves(u):
    lo = lax.bitcast_convert_type(u << 16, F32).astype(BF16)
    hi = lax.bitcast_convert_type(u & jnp.uint32(0xFFFF0000), F32).astype(BF16)
    return jnp.concatenate([lo, hi], axis=1)


def _ada_kernel(c_ref, w_ref, b_ref, lq1_ref, lk1_ref, lq2_ref, lk2_ref, mod_ref, lam_ref, *, lam_init):
    c = c_ref[...]
    sc = c * _sigmoid(c)
    mod_ref[...] = jnp.dot(sc, w_ref[...], precision=HIGHEST, preferred_element_type=F32) + b_ref[...]
    s1 = jnp.sum(lq1_ref[...] * lk1_ref[...], axis=-1, keepdims=True)
    s2 = jnp.sum(lq2_ref[...] * lk2_ref[...], axis=-1, keepdims=True)
    lam = jnp.exp(s1) - jnp.exp(s2) + lam_init
    lam_ref[...] = jnp.broadcast_to(lam, lam_ref.shape)


def _ada(c, w_ada, b_ada, lq1, lk1, lq2, lk2, lam_init):
    B, D = c.shape
    cols = w_ada.shape[1]
    bc = D
    vec = pl.BlockSpec((1, ATTN_HEAD_DIM), lambda j: (0, 0))
    return pl.pallas_call(
        functools.partial(_ada_kernel, lam_init=lam_init),
        name="ada",
        grid=(cols // bc,),
        in_specs=[
            pl.BlockSpec((B, D), lambda j: (0, 0)),
            pl.BlockSpec((D, bc), lambda j: (0, j)),
            pl.BlockSpec((1, bc), lambda j: (0, j)),
            vec, vec, vec, vec,
        ],
        out_specs=[
            pl.BlockSpec((B, bc), lambda j: (0, j)),
            pl.BlockSpec((V7X_SUBLANES, V7X_LANES), lambda j: (0, 0)),
        ],
        out_shape=[
            jax.ShapeDtypeStruct((B, cols), F32),
            jax.ShapeDtypeStruct((V7X_SUBLANES, V7X_LANES), F32),
        ],
        compiler_params=pltpu.CompilerParams(dimension_semantics=("arbitrary",)),
    )(c, w_ada, b_ada, lq1, lk1, lq2, lk2)


def _group_rms_scale(t, lo):
    sq = t * t
    s1 = jnp.sum(jnp.where(lo, sq, 0.0), axis=-1, keepdims=True)
    s2 = jnp.sum(jnp.where(lo, 0.0, sq), axis=-1, keepdims=True)
    r1 = lax.rsqrt(s1 * (1.0 / ATTN_HEAD_DIM) + EPS)
    r2 = lax.rsqrt(s2 * (1.0 / ATTN_HEAD_DIM) + EPS)
    return jnp.where(lo, r1, r2)


def _inproj_kernel(x_ref, mod_ref, g1_ref, w_ref, qg_ref, kg_ref, hglu_ref, q_ref, k_ref, v_ref):
    x = x_ref[...]
    ms = jnp.mean(x * x, axis=-1, keepdims=True)
    shift = mod_ref[0:1, :]
    scale = mod_ref[1:2, :]
    h = ((x * lax.rsqrt(ms + EPS)) * g1_ref[...]) * (1.0 + scale) + shift
    hb = h.astype(BF16)
    c2 = 2 * CONV_CH
    ag = jnp.dot(hb, w_ref[:, 0:c2], preferred_element_type=F32)
    hglu_ref[...] = ag[:, :CONV_CH] * _sigmoid(ag[:, CONV_CH:])
    lo = lax.broadcasted_iota(I32, (1, V7X_LANES), 1) < ATTN_HEAD_DIM
    for src_col, g_ref, o_ref in ((c2, qg_ref, q_ref), (c2 + ATTN_WIDTH, kg_ref, k_ref)):
        t = jnp.dot(hb, w_ref[:, src_col:src_col + ATTN_WIDTH], preferred_element_type=F32)
        for hd in range(ATTN_HEADS):
            sl = slice(hd * V7X_LANES, (hd + 1) * V7X_LANES)
            th = t[:, sl]
            o_ref[:, sl] = (th * _group_rms_scale(th, lo) * g_ref[:, sl]).astype(BF16)
    v0 = c2 + 2 * ATTN_WIDTH
    v_ref[...] = jnp.dot(hb, w_ref[:, v0:v0 + ATTN_WIDTH], preferred_element_type=F32).astype(BF16)


def _inproj(x2d, mod3, g1, w_in_b, qg, kg, seq):
    N, D = x2d.shape
    tm = INPROJ_ROWS
    steps_per_seq = seq // tm
    cols = w_in_b.shape[1]
    row = lambda i: (i, 0)
    const = lambda i: (0, 0)
    return pl.pallas_call(
        _inproj_kernel,
        name="inproj",
        grid=(N // tm,),
        in_specs=[
            pl.BlockSpec((tm, D), row),
            pl.BlockSpec((None, N_MOD, D), lambda i: (i // steps_per_seq, 0, 0)),
            pl.BlockSpec((1, D), const),
            pl.BlockSpec((D, cols), const),
            pl.BlockSpec((1, ATTN_WIDTH), const),
            pl.BlockSpec((1, ATTN_WIDTH), const),
        ],
        out_specs=[
            pl.BlockSpec((tm, CONV_CH), row),
            pl.BlockSpec((tm, ATTN_WIDTH), row),
            pl.BlockSpec((tm, ATTN_WIDTH), row),
            pl.BlockSpec((tm, ATTN_WIDTH), row),
        ],
        out_shape=[
            jax.ShapeDtypeStruct((N, CONV_CH), F32),
            jax.ShapeDtypeStruct((N, ATTN_WIDTH), BF16),
            jax.ShapeDtypeStruct((N, ATTN_WIDTH), BF16),
            jax.ShapeDtypeStruct((N, ATTN_WIDTH), BF16),
        ],
        compiler_params=pltpu.CompilerParams(
            dimension_semantics=("parallel",), vmem_limit_bytes=48 * 1024 * 1024),
    )(x2d, mod3, g1, w_in_b, qg, kg)


def _attn_kernel(lam_ref, slope_ref, q_ref, k_ref, v_ref, sg_ref, o_ref, *, tq, ts, lam_init):
    hd = pl.program_id(1)
    i = pl.program_id(2)
    lam = lam_ref[0]
    slope = slope_ref[hd]
    k = k_ref[...]
    v = v_ref[...]
    S = k.shape[0]
    lo = lax.broadcasted_iota(I32, (1, V7X_LANES), 1) < ATTN_HEAD_DIM
    kpos = lax.broadcasted_iota(I32, (1, S), 1).astype(F32) * slope

    def softmax_parts(qc, bias):
        s = lax.dot_general(qc, k, _nt_dims(), preferred_element_type=F32) - bias
        m = jnp.max(s, axis=-1, keepdims=True)
        p = jnp.exp2(s - m)
        return p, jnp.sum(p, axis=-1, keepdims=True)

    for sub in range(tq // ts):
        rows = slice(sub * ts, (sub + 1) * ts)
        q = q_ref[rows, :]
        zero = jnp.zeros_like(q)
        qpos = (i * tq + sub * ts + lax.broadcasted_iota(I32, (ts, 1), 0)).astype(F32) * slope
        bias = jnp.abs(kpos - qpos)
        p1, l1 = softmax_parts(jnp.where(lo, q, zero), bias)
        o1 = jnp.dot(p1.astype(BF16), v, preferred_element_type=F32) * (1.0 / l1)
        p2, l2 = softmax_parts(jnp.where(lo, zero, q), bias)
        o2 = jnp.dot(p2.astype(BF16), v, preferred_element_type=F32) * (lam / l2)
        o = o1 - o2
        ms = jnp.mean(o * o, axis=-1, keepdims=True)
        o = (o * lax.rsqrt(ms + EPS)) * sg_ref[...] * (1.0 - lam_init)
        o_ref[rows, :] = o.astype(BF16)


def _attention(lam1, slopes, q, k, v, sg, batch, seq, lam_init):
    tq = ATTN_Q_ROWS
    nq = seq // tq
    smem = pl.BlockSpec(memory_space=pltpu.SMEM)
    return pl.pallas_call(
        functools.partial(_attn_kernel, tq=tq, ts=ATTN_SUB_ROWS, lam_init=lam_init),
        name="attn",
        grid=(batch, ATTN_HEADS, nq),
        in_specs=[
            smem, smem,
            pl.BlockSpec((tq, ATTN_V_DIM), lambda b, h, i: (b * nq + i, h)),
            pl.BlockSpec((seq, ATTN_V_DIM), lambda b, h, i: (b, h)),
            pl.BlockSpec((seq, ATTN_V_DIM), lambda b, h, i: (b, h)),
            pl.BlockSpec((1, ATTN_V_DIM), lambda b, h, i: (0, 0)),
        ],
        out_specs=pl.BlockSpec((tq, ATTN_V_DIM), lambda b, h, i: (b * nq + i, h)),
        out_shape=jax.ShapeDtypeStruct(q.shape, BF16),
        compiler_params=pltpu.CompilerParams(
            dimension_semantics=("parallel", "parallel", "parallel"),
            vmem_limit_bytes=48 * 1024 * 1024),
    )(lam1, slopes, q, k, v, sg)


def _mixout_kernel(prev_ref, main_ref, next_ref, attn_ref, x_ref, mod_ref, cw_ref, cb_ref, lg_ref, lb_ref,
                   wo_ref, x1_ref, win_ref, sh_ref, *, tiles_per_seq):
    T = main_ref.shape[0]
    j = lax.rem(pl.program_id(0), tiles_per_seq)
    H = CONV_HALO
    win_ref[0:H, :] = jnp.where(j == 0, 0.0, prev_ref[...])
    win_ref[H:H + T, :] = main_ref[...]
    win_ref[H + T:H + T + H, :] = jnp.where(j == tiles_per_seq - 1, 0.0, next_ref[...])
    rows = sh_ref.shape[1]
    for r in range(1, V7X_SUBLANES):
        sh_ref[r] = win_ref[r:r + rows, :]
    acc = jnp.broadcast_to(cb_ref[...], (T, CONV_CH))
    off = H - CONV_KERNEL // 2
    for tap in range(CONV_KERNEL):
        a, r = divmod(off + tap, V7X_SUBLANES)
        src = win_ref if r == 0 else sh_ref.at[r]
        acc = acc + cw_ref[tap:tap + 1, :] * src[a * V7X_SUBLANES:a * V7X_SUBLANES + T, :]
    mu = jnp.mean(acc, axis=-1, keepdims=True)
    xc = acc - mu
    var = jnp.mean(xc * xc, axis=-1, keepdims=True)
    y = (xc * lax.rsqrt(var + EPS)) * lg_ref[...] + lb_ref[...]
    y = y * _sigmoid(y)
    mix = jnp.dot(y.astype(BF16), wo_ref[0:CONV_CH, :], preferred_element_type=F32)
    mix = mix + jnp.dot(attn_ref[...], wo_ref[CONV_CH:CONV_CH + ATTN_WIDTH, :], preferred_element_type=F32)
    x1_ref[...] = x_ref[...] + mod_ref[2:3, :] * mix


def _mixout(hglu, attn, x2d, mod3, cw, cb, lg, lb, wo_b, seq):
    N, D = x2d.shape
    T = TOK_TILE
    tiles_per_seq = seq // T
    hpt = T // CONV_HALO
    n_halo = N // CONV_HALO
    row = lambda i: (i, 0)
    const = lambda i: (0, 0)
    return pl.pallas_call(
        functools.partial(_mixout_kernel, tiles_per_seq=tiles_per_seq),
        name="mixout",
        grid=(N // T,),
        in_specs=[
            pl.BlockSpec((CONV_HALO, CONV_CH), lambda i: (jnp.maximum(i * hpt - 1, 0), 0)),
            pl.BlockSpec((T, CONV_CH), row),
            pl.BlockSpec((CONV_HALO, CONV_CH), lambda i: (jnp.minimum((i + 1) * hpt, n_halo - 1), 0)),
            pl.BlockSpec((T, ATTN_WIDTH), row),
            pl.BlockSpec((T, D), row),
            pl.BlockSpec((None, N_MOD, D), lambda i: (i // tiles_per_seq, 0, 0)),
            pl.BlockSpec(cw.shape, const),
            pl.BlockSpec((1, CONV_CH), const),
            pl.BlockSpec((1, CONV_CH), const),
            pl.BlockSpec((1, CONV_CH), const),
            pl.BlockSpec(wo_b.shape, const),
        ],
        out_specs=pl.BlockSpec((T, D), row),
        out_shape=jax.ShapeDtypeStruct((N, D), F32),
        scratch_shapes=[
            pltpu.VMEM((T + 2 * CONV_HALO, CONV_CH), F32),
            pltpu.VMEM((V7X_SUBLANES, T + 2 * CONV_HALO - V7X_SUBLANES, CONV_CH), F32),
        ],
        compiler_params=pltpu.CompilerParams(
            dimension_semantics=("parallel",), vmem_limit_bytes=48 * 1024 * 1024),
    )(hglu, hglu, hglu, attn, x2d, mod3, cw, cb, lg, lb, wo_b)


def _route_kernel(x1_ref, mod_ref, g2_ref, wr_ref, br_ref, xs_ref, rt_ref, ce_ref, *, n_steps):
    i = pl.program_id(0)

    @pl.when(i < n_steps)
    def _():
        _route_tiles(x1_ref, mod_ref, g2_ref, wr_ref, br_ref, xs_ref, rt_ref, ce_ref)

    @pl.when(i >= n_steps)
    def _():
        xs_ref[...] = jnp.zeros(xs_ref.shape, U32)
        rt_ref[...] = jnp.zeros(rt_ref.shape, F32)
        ce_ref[...] = jnp.full(ce_ref.shape, N_EXPERTS, I32)


def _route_tiles(x1_ref, mod_ref, g2_ref, wr_ref, br_ref, xs_ref, rt_ref, ce_ref):
    G, T, E, L = ROUTE_TILES_PER_STEP, TOK_TILE, N_EXPERTS, SORT_ROWS
    W = G * T
    x1 = x1_ref[...]
    ms = jnp.mean(x1 * x1, axis=-1, keepdims=True)
    h2 = ((x1 * lax.rsqrt(ms + EPS)) * g2_ref[...]) * (1.0 + mod_ref[4:5, :]) + mod_ref[3:4, :]
    h2b = h2.astype(BF16)
    logits = lax.dot_general(wr_ref[...], h2, _nt_dims(), precision=HIGHEST,
                             preferred_element_type=F32) + br_ref[...]
    e_iota = lax.broadcasted_iota(I32, (E, W), 0)
    sels, tops = [], []
    l = logits
    for _ in range(TOP_K):
        m = jnp.max(l, axis=0, keepdims=True)
        idx = jnp.min(jnp.where(l == m, e_iota, E), axis=0, keepdims=True)
        sel = e_iota == idx
        l = jnp.where(sel, -jnp.inf, l)
        sels.append(sel)
        tops.append(m)
    ws = [jnp.exp(m - tops[0]) for m in tops]
    den = ws[0] + ws[1] + ws[2] + ws[3]
    gates = [w / den for w in ws]
    multi = jnp.zeros((E, W), F32)
    for sel in sels:
        multi = multi + jnp.where(sel, 1.0, 0.0)
    multi_b = multi.astype(BF16)
    r_iota = lax.broadcasted_iota(I32, (T, T + V7X_LANES), 0)
    c_iota = lax.broadcasted_iota(I32, (T, T + V7X_LANES), 1)
    tri = jnp.where((r_iota < c_iota) | (c_iota >= T), 1.0, 0.0).astype(BF16)
    rks = [jnp.dot(multi_b[:, h * T:(h + 1) * T], tri, preferred_element_type=F32) for h in range(G)]
    cnt = jnp.concatenate([rk[:, T:T + V7X_LANES] for rk in rks], axis=1)
    n8 = jnp.floor((cnt + (CHUNK - 1)) * (1.0 / CHUNK))
    e_iota_l = lax.broadcasted_iota(I32, n8.shape, 0)
    lo8 = jnp.zeros(n8.shape, F32)
    for e in range(E - 1):
        lo8 = lo8 + jnp.where(e_iota_l > e, n8[e:e + 1, :], 0.0)
    s_iota = lax.broadcasted_iota(I32, (E, ce_ref.shape[2]), 1).astype(F32)
    j_iota = lax.broadcasted_iota(I32, (L, T), 0).astype(F32)
    for h in range(G):
        cols = slice(h * T, (h + 1) * T)
        lo_h = lo8[:, h * V7X_LANES:h * V7X_LANES + 1]
        n_h = n8[:, h * V7X_LANES:h * V7X_LANES + 1]
        base = lo_h * float(CHUNK) + rks[h][:, :T]
        dests = [jnp.sum(jnp.where(sel[:, cols], base, 0.0), axis=0, keepdims=True) for sel in sels]
        for k in range(TOP_K):
            rt_ref[h, k:k + 1, :] = dests[k]
            rt_ref[h, TOP_K + k:TOP_K + k + 1, :] = gates[k][:, cols]
        ce_ref[h] = jnp.sum(jnp.where(lo_h + n_h <= s_iota, 1.0, 0.0), axis=0, keepdims=True).astype(I32)
        pm = jnp.zeros((L, T), F32)
        for d in dests:
            pm = pm + jnp.where(j_iota == d, 1.0, 0.0)
        xs_ref[h * L:(h + 1) * L, :] = _pack_halves(
            jnp.dot(pm.astype(BF16), h2b[h * T:(h + 1) * T, :], preferred_element_type=F32), is_bf16_exact=True)


def _route(x1, mod3, g2, wr_t, br, seq, trash_tiles):
    N, D = x1.shape
    T = TOK_TILE
    tiles_per_seq = seq // T
    n_tiles = N // T
    n_ext = n_tiles + trash_tiles
    L = SORT_ROWS
    G = ROUTE_TILES_PER_STEP
    assert tiles_per_seq % G == 0 and n_ext % G == 0
    n_steps = n_tiles // G
    ce_w = _round_up(SORT_CHUNKS, V7X_LANES)
    const = lambda i: (0, 0)
    real = lambda i: jnp.minimum(i, n_steps - 1)
    return pl.pallas_call(
        functools.partial(_route_kernel, n_steps=n_steps),
        name="route",
        grid=(n_ext // G,),
        in_specs=[
            pl.BlockSpec((G * T, D), lambda i: (real(i), 0)),
            pl.BlockSpec((None, N_MOD, D), lambda i: (real(i) // (tiles_per_seq // G), 0, 0)),
            pl.BlockSpec((1, D), const),
            pl.BlockSpec(wr_t.shape, const),
            pl.BlockSpec(br.shape, const),
        ],
        out_specs=[
            pl.BlockSpec((G * L, D // 2), lambda i: (i, 0)),
            pl.BlockSpec((G, 2 * TOP_K, T), lambda i: (i, 0, 0)),
            pl.BlockSpec((G, 1, ce_w), lambda i: (i, 0, 0)),
        ],
        out_shape=[
            jax.ShapeDtypeStruct((n_ext * L, D // 2), U32),
            jax.ShapeDtypeStruct((n_ext, 2 * TOP_K, T), F32),
            jax.ShapeDtypeStruct((n_ext, 1, ce_w), I32),
        ],
        compiler_params=pltpu.CompilerParams(
            dimension_semantics=("parallel",), vmem_limit_bytes=48 * 1024 * 1024),
    )(x1, mod3, g2, wr_t, br)


def _expert_kernel(be_ref, nb_ref, slot_ref, xs_hbm, wgu_ref, bgu_ref, wd_ref, bd_ref, ys_hbm,
                   xbuf, obuf, wgu_b, wd_b, sem_in, sem_out):
    b = pl.program_id(0)
    nb = nb_ref[0]
    dff = wd_ref.shape[0]

    def gather_copy(blk, slot, j):
        chunk = slot_ref[blk * CHUNKS_PER_BLOCK + j]
        return pltpu.make_async_copy(xs_hbm.at[chunk], xbuf.at[slot, pl.ds(j * CHUNK, CHUNK)], sem_in.at[slot])

    def scatter_copy(blk, slot, j):
        chunk = slot_ref[blk * CHUNKS_PER_BLOCK + j]
        return pltpu.make_async_copy(obuf.at[slot, pl.ds(j * CHUNK, CHUNK)], ys_hbm.at[chunk], sem_out.at[slot])

    def start_gather(blk, slot):
        for j in range(CHUNKS_PER_BLOCK):
            gather_copy(blk, slot, j).start()

    def wait_gather(blk, slot):
        for j in range(CHUNKS_PER_BLOCK):
            gather_copy(blk, slot, j).wait()

    def start_scatter(blk, slot):
        for j in range(CHUNKS_PER_BLOCK):
            scatter_copy(blk, slot, j).start()

    def wait_scatter(blk, slot):
        for j in range(CHUNKS_PER_BLOCK):
            scatter_copy(blk, slot, j).wait()

    @pl.when(b < nb)
    def _():
        slot = lax.rem(b, 2)

        @pl.when(b == 0)
        def _():
            start_gather(0, 0)

        @pl.when(b + 1 < nb)
        def _():
            start_gather(b + 1, 1 - slot)

        @pl.when((b == 0) | (be_ref[b] != be_ref[jnp.maximum(b - 1, 0)]))
        def _():
            wgu_b[...] = wgu_ref[...].astype(BF16)
            wd_b[...] = wd_ref[...].astype(BF16)

        wait_gather(b, slot)

        @pl.when(b >= 2)
        def _():
            wait_scatter(b - 2, slot)

        xb = _unpack_halves(xbuf[slot])
        gu = jnp.dot(xb, wgu_b[...], preferred_element_type=F32) + bgu_ref[...]
        g = jnp.minimum(gu[:, :dff], SWIGLU_LIMIT)
        u = jnp.clip(gu[:, dff:], -SWIGLU_LIMIT, SWIGLU_LIMIT)
        y = (u + 1.0) * (g * _sigmoid(SWIGLU_ALPHA * g))
        out = jnp.dot(y.astype(BF16), wd_b[...], preferred_element_type=F32) + bd_ref[...]
        obuf[slot] = _pack_halves(out, is_bf16_exact=False)
        start_scatter(b, slot)

        @pl.when(b == nb - 1)
        def _():
            wait_scatter(b, slot)

            @pl.when(b >= 1)
            def _():
                wait_scatter(b - 1, 1 - slot)


def _experts(blk_expert, nb_total, slots, xs, wgu, bgu, wd, bd):
    nb_max = blk_expert.shape[0]
    half = xs.shape[1]
    D = 2 * half
    f2 = wgu.shape[2]
    dff = wd.shape[1]
    chunks = xs.reshape(xs.shape[0] // CHUNK, CHUNK, half)
    wmap = lambda b, be, nb, sl: (be[b], 0, 0)
    grid_spec = pltpu.PrefetchScalarGridSpec(
        num_scalar_prefetch=3,
        grid=(nb_max,),
        in_specs=[
            pl.BlockSpec(memory_space=pl.ANY),
            pl.BlockSpec((None, D, f2), wmap),
            pl.BlockSpec((None, 1, f2), wmap),
            pl.BlockSpec((None, dff, D), wmap),
            pl.BlockSpec((None, 1, D), wmap),
        ],
        out_specs=pl.BlockSpec(memory_space=pl.ANY),
        scratch_shapes=[
            pltpu.VMEM((2, BLOCK_ROWS, half), U32),
            pltpu.VMEM((2, BLOCK_ROWS, half), U32),
            pltpu.VMEM((D, f2), BF16),
            pltpu.VMEM((dff, D), BF16),
            pltpu.SemaphoreType.DMA((2,)),
            pltpu.SemaphoreType.DMA((2,)),
        ],
    )
    ys = pl.pallas_call(
        _expert_kernel,
        name="experts",
        grid_spec=grid_spec,
        out_shape=jax.ShapeDtypeStruct(chunks.shape, U32),
        input_output_aliases={3: 0},
        compiler_params=pltpu.CompilerParams(
            dimension_semantics=("arbitrary",), vmem_limit_bytes=56 * 1024 * 1024),
    )(blk_expert, nb_total, slots, chunks, wgu, bgu, wd, bd)
    return ys.reshape(xs.shape)


def _combine_kernel(ys_ref, rt_ref, x1_ref, mod_ref, x2_ref):
    T, L = TOK_TILE, SORT_ROWS
    j_iota = lax.broadcasted_iota(I32, (T, L), 1).astype(F32)
    for h in range(COMBINE_TILES_PER_STEP):
        rows = slice(h * T, (h + 1) * T)
        wm = jnp.zeros((T, L), F32)
        for k in range(TOP_K):
            wm = wm + jnp.where(j_iota == rt_ref[rows, k:k + 1], rt_ref[rows, TOP_K + k:TOP_K + k + 1], 0.0)
        y = jnp.dot(wm.astype(BF16), _unpack_halves(ys_ref[h * L:(h + 1) * L, :]), preferred_element_type=F32)
        x2_ref[rows, :] = x1_ref[rows, :] + mod_ref[5:6, :] * y


def _combine(ys, rt, x1, mod3, seq):
    N, D = x1.shape
    G = COMBINE_TILES_PER_STEP
    T = G * TOK_TILE
    tiles_per_seq = seq // T
    L = G * SORT_ROWS
    assert seq % T == 0
    return pl.pallas_call(
        _combine_kernel,
        name="combine",
        grid=(N // T,),
        in_specs=[
            pl.BlockSpec((L, D // 2), lambda i: (i, 0)),
            pl.BlockSpec((T, 2 * TOP_K), lambda i: (i, 0)),
            pl.BlockSpec((T, D), lambda i: (i, 0)),
            pl.BlockSpec((None, N_MOD, D), lambda i: (i // tiles_per_seq, 0, 0)),
        ],
        out_specs=pl.BlockSpec((T, D), lambda i: (i, 0)),
        out_shape=jax.ShapeDtypeStruct((N, D), F32),
        compiler_params=pltpu.CompilerParams(
            dimension_semantics=("parallel",), vmem_limit_bytes=48 * 1024 * 1024),
    )(ys, rt, x1, mod3)


def _block_tables(chunk_expert, n_tiles):
    E, cpb = N_EXPERTS, CHUNKS_PER_BLOCK
    keys = chunk_expert[:, 0, :SORT_CHUNKS].reshape(-1)
    order = jnp.argsort(keys, stable=True).astype(I32)
    cc = jnp.sum((keys[:, None] == jnp.arange(E, dtype=I32)[None, :]).astype(I32), axis=0)
    cend = jnp.cumsum(cc)
    cstart = cend - cc
    nbk = (cc + cpb - 1) // cpb
    bend = jnp.cumsum(nbk)
    bstart = bend - nbk
    nb_total = bend[-1:]
    nb_max = -(-(n_tiles * USED_CHUNKS_MAX) // cpb) + E
    bidx = jnp.arange(nb_max, dtype=I32)
    be = jnp.minimum(jnp.sum((bidx[:, None] >= bend[None, :]).astype(I32), axis=1), E - 1)
    j = jnp.arange(cpb, dtype=I32)[None, :]
    pos = cstart[be][:, None] + (bidx - bstart[be])[:, None] * cpb + j
    valid = (pos < cend[be][:, None]) & (bidx < nb_total[0])[:, None]
    src = order[jnp.clip(pos, 0, order.shape[0] - 1)]
    trash = be[:, None] * cpb + j
    slots = jnp.where(valid, src, n_tiles * SORT_CHUNKS + trash).astype(I32)
    return be, nb_total.astype(I32), slots.reshape(-1)


def kernel(x, c, w_ada, b_ada, norm1_g, w_in, q_norm_g, k_norm_g, lambda_q1, lambda_k1, lambda_q2,
           lambda_k2, subln_g, conv_w, conv_b, conv_ln_g, conv_ln_b, w_out, norm2_g, w_router, b_router,
           w_gate_up, b_gate_up, w_down, b_down):
    B, S, D = x.shape
    N = B * S
    assert S % INPROJ_ROWS == 0 and S % ATTN_Q_ROWS == 0 and S % TOK_TILE == 0
    n_tiles = N // TOK_TILE
    trash_tiles = _round_up(-(-(N_EXPERTS * CHUNKS_PER_BLOCK) // SORT_CHUNKS), ROUTE_TILES_PER_STEP)
    log2e = math.log2(math.e)
    slopes = jnp.exp2(-8.0 * jnp.arange(1, ATTN_HEADS + 1, dtype=F32) / ATTN_HEADS) * log2e
    xt = x.reshape(N, D)
    for l in range(w_ada.shape[0]):
        lam_init = 0.8 - 0.6 * math.exp(-0.3 * l)
        row = lambda a: a[l][None, :]
        mod, lam = _ada(c, w_ada[l], row(b_ada), row(lambda_q1), row(lambda_k1), row(lambda_q2),
                        row(lambda_k2), lam_init)
        mod3 = mod.reshape(B, N_MOD, D)
        qg = jnp.tile(q_norm_g[l], 2 * ATTN_HEADS)[None, :] * (ATTN_HEAD_DIM ** -0.5 * log2e)
        kg = jnp.tile(k_norm_g[l], 2 * ATTN_HEADS)[None, :]
        hglu, q, k, v = _inproj(xt, mod3, row(norm1_g), w_in[l].astype(BF16), qg, kg, S)
        attn = _attention(lam[0, 0:1], slopes, q, k, v, row(subln_g), B, S, lam_init)
        cw = jnp.pad(conv_w[l], ((0, 1), (0, 0)))
        x1 = _mixout(hglu, attn, xt, mod3, cw, row(conv_b), row(conv_ln_g), row(conv_ln_b),
                     w_out[l].astype(BF16), S)
        xs, rt, ce = _route(x1, mod3, row(norm2_g), w_router[l].T, b_router[l][:, None], S, trash_tiles)
        be, nb_total, slots = _block_tables(ce[:n_tiles], n_tiles)
        ys = _experts(be, nb_total, slots, xs, w_gate_up[l], b_gate_up[l][:, None, :],
                      w_down[l], b_down[l][:, None, :])
        rt_tok = rt[:n_tiles].transpose(0, 2, 1).reshape(N, 2 * TOP_K)
        xt = _combine(ys, rt_tok, x1, mod3, S)
    return xt.reshape(B, S, D)
```

```python
import functools
import math

import jax
import jax.numpy as jnp
from jax import lax
from jax.experimental import pallas as pl
from jax.experimental.pallas import tpu as pltpu

F32 = jnp.float32
BF16 = jnp.bfloat16
I32 = jnp.int32
U32 = jnp.uint32
HIGHEST = lax.Precision.HIGHEST

CONV_CH = 512
CONV_KERNEL = 31
CONV_HALO = 16
ATTN_HEADS = 4
ATTN_HEAD_DIM = 64
ATTN_V_DIM = 2 * ATTN_HEAD_DIM
ATTN_WIDTH = ATTN_HEADS * ATTN_V_DIM
N_EXPERTS = 32
TOP_K = 4
SWIGLU_LIMIT = 7.0
SWIGLU_ALPHA = 1.702
N_MOD = 6
EPS = 1e-5

V7X_LANES = 128
V7X_SUBLANES = 8

INPROJ_ROWS = 1024
ATTN_SUB_ROWS = 256
TOK_TILE = 256
COMBINE_TILES_PER_STEP = 2
ROUTE_TILES_PER_STEP = 4
CHUNK = V7X_SUBLANES
BLOCK_ROWS = 512
CHUNKS_PER_BLOCK = BLOCK_ROWS // CHUNK


def _round_up(a, b):
    return (a + b - 1) // b * b


USED_CHUNKS_MAX = (TOP_K * TOK_TILE + N_EXPERTS * (CHUNK - 1)) // CHUNK
SORT_ROWS = _round_up(USED_CHUNKS_MAX * CHUNK, V7X_LANES)
SORT_CHUNKS = SORT_ROWS // CHUNK


def _sigmoid(v):
    return 1.0 / (1.0 + jnp.exp(-v))


def _nt_dims():
    return (((1,), (1,)), ((), ()))


def _pack_halves(v, *, is_bf16_exact):
    c = v.shape[1] // 2
    bits = lax.bitcast_convert_type(v, U32)
    if not is_bf16_exact:
        bits = bits + jnp.uint32(0x7FFF) + ((bits >> 16) & jnp.uint32(1))
    return (bits[:, :c] >> 16) | (bits[:, c:] & jnp.uint32(0xFFFF0000))


def _unpack_halves(u):
    lo = lax.bitcast_convert_type(u << 16, F32).astype(BF16)
    hi = lax.bitcast_convert_type(u & jnp.uint32(0xFFFF0000), F32).astype(BF16)
    return jnp.concatenate([lo, hi], axis=1)


def _ada_kernel(c_ref, w_ref, b_ref, lq1_ref, lk1_ref, lq2_ref, lk2_ref, mod_ref, lam_ref, *, lam_init):
    c = c_ref[...]
    sc = c * _sigmoid(c)
    mod_ref[...] = jnp.dot(sc, w_ref[...], precision=HIGHEST, preferred_element_type=F32) + b_ref[...]
    s1 = jnp.sum(lq1_ref[...] * lk1_ref[...], axis=-1, keepdims=True)
    s2 = jnp.sum(lq2_ref[...] * lk2_ref[...], axis=-1, keepdims=True)
    lam = jnp.exp(s1) - jnp.exp(s2) + lam_init
    lam_ref[...] = jnp.broadcast_to(lam, lam_ref.shape)


def _ada(c, w_ada, b_ada, lq1, lk1, lq2, lk2, lam_init):
    B, D = c.shape
    cols = w_ada.shape[1]
    bc = D
    vec = pl.BlockSpec((1, ATTN_HEAD_DIM), lambda j: (0, 0))
    return pl.pallas_call(
        functools.partial(_ada_kernel, lam_init=lam_init),
        name="ada",
        grid=(cols // bc,),
        in_specs=[
            pl.BlockSpec((B, D), lambda j: (0, 0)),
            pl.BlockSpec((D, bc), lambda j: (0, j)),
            pl.BlockSpec((1, bc), lambda j: (0, j)),
            vec, vec, vec, vec,
        ],
        out_specs=[
            pl.BlockSpec((B, bc), lambda j: (0, j)),
            pl.BlockSpec((V7X_SUBLANES, V7X_LANES), lambda j: (0, 0)),
        ],
        out_shape=[
            jax.ShapeDtypeStruct((B, cols), F32),
            jax.ShapeDtypeStruct((V7X_SUBLANES, V7X_LANES), F32),
        ],
        compiler_params=pltpu.CompilerParams(dimension_semantics=("arbitrary",)),
    )(c, w_ada, b_ada, lq1, lk1, lq2, lk2)


def _group_rms_scale(t, lo):
    sq = t * t
    s1 = jnp.sum(jnp.where(lo, sq, 0.0), axis=-1, keepdims=True)
    s2 = jnp.sum(jnp.where(lo, 0.0, sq), axis=-1, keepdims=True)
    r1 = lax.rsqrt(s1 * (1.0 / ATTN_HEAD_DIM) + EPS)
    r2 = lax.rsqrt(s2 * (1.0 / ATTN_HEAD_DIM) + EPS)
    return jnp.where(lo, r1, r2)


def _inproj_kernel(x_ref, mod_ref, g1_ref, w_ref, qg_ref, kg_ref, hglu_ref, q_ref, k_ref, v_ref):
    x = x_ref[...]
    ms = jnp.mean(x * x, axis=-1, keepdims=True)
    shift = mod_ref[0:1, :]
    scale = mod_ref[1:2, :]
    h = ((x * lax.rsqrt(ms + EPS)) * g1_ref[...]) * (1.0 + scale) + shift
    hb = h.astype(BF16)
    c2 = 2 * CONV_CH
    ag = jnp.dot(hb, w_ref[:, 0:c2], preferred_element_type=F32)
    hglu_ref[...] = ag[:, :CONV_CH] * _sigmoid(ag[:, CONV_CH:])
    lo = lax.broadcasted_iota(I32, (1, V7X_LANES), 1) < ATTN_HEAD_DIM
    for src_col, g_ref, o_ref in ((c2, qg_ref, q_ref), (c2 + ATTN_WIDTH, kg_ref, k_ref)):
        t = jnp.dot(hb, w_ref[:, src_col:src_col + ATTN_WIDTH], preferred_element_type=F32)
        for hd in range(ATTN_HEADS):
            sl = slice(hd * V7X_LANES, (hd + 1) * V7X_LANES)
            th = t[:, sl]
            o_ref[:, sl] = (th * _group_rms_scale(th, lo) * g_ref[:, sl]).astype(BF16)
    v0 = c2 + 2 * ATTN_WIDTH
    v_ref[...] = jnp.dot(hb, w_ref[:, v0:v0 + ATTN_WIDTH], preferred_element_type=F32).astype(BF16)


def _inproj(x2d, mod3, g1, w_in_b, qg, kg, seq):
    N, D = x2d.shape
    tm = INPROJ_ROWS
    steps_per_seq = seq // tm
    cols = w_in_b.shape[1]
    row = lambda i: (i, 0)
    const = lambda i: (0, 0)
    return pl.pallas_call(
        _inproj_kernel,
        name="inproj",
        grid=(N // tm,),
        in_specs=[
            pl.BlockSpec((tm, D), row),
            pl.BlockSpec((None, N_MOD, D), lambda i: (i // steps_per_seq, 0, 0)),
            pl.BlockSpec((1, D), const),
            pl.BlockSpec((D, cols), const),
            pl.BlockSpec((1, ATTN_WIDTH), const),
            pl.BlockSpec((1, ATTN_WIDTH), const),
        ],
        out_specs=[
            pl.BlockSpec((tm, CONV_CH), row),
            pl.BlockSpec((tm, ATTN_WIDTH), row),
            pl.BlockSpec((tm, ATTN_WIDTH), row),
            pl.BlockSpec((tm, ATTN_WIDTH), row),
        ],
        out_shape=[
            jax.ShapeDtypeStruct((N, CONV_CH), F32),
            jax.ShapeDtypeStruct((N, ATTN_WIDTH), BF16),
            jax.ShapeDtypeStruct((N, ATTN_WIDTH), BF16),
            jax.ShapeDtypeStruct((N, ATTN_WIDTH), BF16),
        ],
        compiler_params=pltpu.CompilerParams(
            dimension_semantics=("parallel",), vmem_limit_bytes=48 * 1024 * 1024),
    )(x2d, mod3, g1, w_in_b, qg, kg)


ATTN_FEAT = 9


def _split3_bf16(x):
    def trunc(v):
        bits = lax.bitcast_convert_type(v, U32) & jnp.uint32(0xFFFF0000)
        return lax.bitcast_convert_type(bits, F32)
    x0 = trunc(x)
    x1 = trunc(x - x0)
    x2 = trunc(x - x0 - x1)
    return x0, x1, x2


def _alibi_features(slopes, seq):
    pos = jnp.arange(seq, dtype=I32)
    a = (pos // 16).astype(F32)
    b = (pos % 16).astype(F32)
    one = jnp.ones((seq,), F32)
    pad = jnp.zeros((seq, V7X_LANES - ATTN_FEAT), F32)
    qfeat = jnp.concatenate([jnp.stack([a, a, a, b, b, b, one, one, one], axis=1), pad], axis=1)
    kfs = []
    for h in range(ATTN_HEADS):
        sl = slopes[h]
        c16 = [-c * one for c in _split3_bf16(16.0 * sl)]
        c1 = [-c * one for c in _split3_bf16(sl)]
        tk = list(_split3_bf16(sl * pos.astype(F32)))
        kfs.append(jnp.concatenate([jnp.stack(c16 + c1 + tk, axis=1), pad], axis=1))
    return qfeat.astype(BF16), jnp.stack(kfs, axis=0).astype(BF16)


def _attn_kernel(lam_ref, slope_ref, q_ref, qf_ref, k_ref, kf_ref, v_ref, sg_ref, o_ref, *, ts, lam_init):
    hd = pl.program_id(1)
    lam = lam_ref[0]
    slope = slope_ref[hd]
    S = k_ref.shape[0]
    lo = lax.broadcasted_iota(I32, (1, V7X_LANES), 1) < ATTN_HEAD_DIM
    rel = lax.broadcasted_iota(I32, (ts, ts), 0) - lax.broadcasted_iota(I32, (ts, ts), 1)
    dbias = jnp.abs(rel).astype(F32) * slope
    k = k_ref[...]
    kf = kf_ref[...]
    v1 = jnp.concatenate([v_ref[...], jnp.ones((S, ATTN_V_DIM), BF16)], axis=1)
    k_left = jnp.concatenate([k, kf], axis=1)
    k_right = jnp.concatenate([k, -kf], axis=1)
    dn = _nt_dims()

    for t in range(S // ts):
        rows = slice(t * ts, (t + 1) * ts)
        q = q_ref[rows, :]
        qf = qf_ref[rows, :]
        zero = jnp.zeros_like(q)
        outs = []
        for qc in (jnp.where(lo, q, zero), jnp.where(lo, zero, q)):
            qa = jnp.concatenate([qc, qf], axis=1)
            parts = [(lax.dot_general(qc, k_ref[rows, :], dn, preferred_element_type=F32) - dbias, v1[rows])]
            if t > 0:
                parts.append((lax.dot_general(qa, k_left[:t * ts], dn, preferred_element_type=F32),
                              v1[:t * ts]))
            if (t + 1) * ts < S:
                parts.append((lax.dot_general(qa, k_right[(t + 1) * ts:], dn, preferred_element_type=F32),
                              v1[(t + 1) * ts:]))
            m = parts[0][0].max(axis=-1, keepdims=True)
            for sp, _ in parts[1:]:
                m = jnp.maximum(m, sp.max(axis=-1, keepdims=True))
            acc = jnp.zeros((ts, 2 * ATTN_V_DIM), F32)
            for sp, vp in parts:
                acc = acc + jnp.dot(jnp.exp2(sp - m).astype(BF16), vp, preferred_element_type=F32)
            outs.append((acc[:, :ATTN_V_DIM], acc[:, ATTN_V_DIM:ATTN_V_DIM + 1]))
        (a1, l1), (a2, l2) = outs
        o = a1 * (1.0 / l1) - a2 * (lam / l2)
        ms = jnp.mean(o * o, axis=-1, keepdims=True)
        o = (o * lax.rsqrt(ms + EPS)) * sg_ref[...] * (1.0 - lam_init)
        o_ref[rows, :] = o.astype(BF16)


def _attention(lam1, slopes, qfeat, kfeat, q, k, v, sg, batch, seq, lam_init):
    smem = pl.BlockSpec(memory_space=pltpu.SMEM)
    seq_blk = lambda b, h: (b, h)
    return pl.pallas_call(
        functools.partial(_attn_kernel, ts=ATTN_SUB_ROWS, lam_init=lam_init),
        name="attn",
        grid=(batch, ATTN_HEADS),
        in_specs=[
            smem, smem,
            pl.BlockSpec((seq, ATTN_V_DIM), seq_blk),
            pl.BlockSpec((seq, V7X_LANES), lambda b, h: (0, 0)),
            pl.BlockSpec((seq, ATTN_V_DIM), seq_blk),
            pl.BlockSpec((None, seq, V7X_LANES), lambda b, h: (h, 0, 0)),
            pl.BlockSpec((seq, ATTN_V_DIM), seq_blk),
            pl.BlockSpec((1, ATTN_V_DIM), lambda b, h: (0, 0)),
        ],
        out_specs=pl.BlockSpec((seq, ATTN_V_DIM), seq_blk),
        out_shape=jax.ShapeDtypeStruct(q.shape, BF16),
        compiler_params=pltpu.CompilerParams(
            dimension_semantics=("parallel", "parallel"), vmem_limit_bytes=48 * 1024 * 1024),
    )(lam1, slopes, q, qfeat, k, kfeat, v, sg)


def _mixout_kernel(prev_ref, main_ref, next_ref, attn_ref, x_ref, mod_ref, cw_ref, cb_ref, lg_ref, lb_ref,
                   wo_ref, x1_ref, win_ref, sh_ref, *, tiles_per_seq):
    T = main_ref.shape[0]
    j = lax.rem(pl.program_id(0), tiles_per_seq)
    H = CONV_HALO
    win_ref[0:H, :] = jnp.where(j == 0, 0.0, prev_ref[...])
    win_ref[H:H + T, :] = main_ref[...]
    win_ref[H + T:H + T + H, :] = jnp.where(j == tiles_per_seq - 1, 0.0, next_ref[...])
    rows = sh_ref.shape[1]
    for r in range(1, V7X_SUBLANES):
        sh_ref[r] = win_ref[r:r + rows, :]
    acc = jnp.broadcast_to(cb_ref[...], (T, CONV_CH))
    off = H - CONV_KERNEL // 2
    for tap in range(CONV_KERNEL):
        a, r = divmod(off + tap, V7X_SUBLANES)
        src = win_ref if r == 0 else sh_ref.at[r]
        acc = acc + cw_ref[tap:tap + 1, :] * src[a * V7X_SUBLANES:a * V7X_SUBLANES + T, :]
    mu = jnp.mean(acc, axis=-1, keepdims=True)
    xc = acc - mu
    var = jnp.mean(xc * xc, axis=-1, keepdims=True)
    y = (xc * lax.rsqrt(var + EPS)) * lg_ref[...] + lb_ref[...]
    y = y * _sigmoid(y)
    mix = jnp.dot(y.astype(BF16), wo_ref[0:CONV_CH, :], preferred_element_type=F32)
    mix = mix + jnp.dot(attn_ref[...], wo_ref[CONV_CH:CONV_CH + ATTN_WIDTH, :], preferred_element_type=F32)
    x1_ref[...] = x_ref[...] + mod_ref[2:3, :] * mix


def _mixout(hglu, attn, x2d, mod3, cw, cb, lg, lb, wo_b, seq):
    N, D = x2d.shape
    T = TOK_TILE
    tiles_per_seq = seq // T
    hpt = T // CONV_HALO
    n_halo = N // CONV_HALO
    row = lambda i: (i, 0)
    const = lambda i: (0, 0)
    return pl.pallas_call(
        functools.partial(_mixout_kernel, tiles_per_seq=tiles_per_seq),
        name="mixout",
        grid=(N // T,),
        in_specs=[
            pl.BlockSpec((CONV_HALO, CONV_CH), lambda i: (jnp.maximum(i * hpt - 1, 0), 0)),
            pl.BlockSpec((T, CONV_CH), row),
            pl.BlockSpec((CONV_HALO, CONV_CH), lambda i: (jnp.minimum((i + 1) * hpt, n_halo - 1), 0)),
            pl.BlockSpec((T, ATTN_WIDTH), row),
            pl.BlockSpec((T, D), row),
            pl.BlockSpec((None, N_MOD, D), lambda i: (i // tiles_per_seq, 0, 0)),
            pl.BlockSpec(cw.shape, const),
            pl.BlockSpec((1, CONV_CH), const),
            pl.BlockSpec((1, CONV_CH), const),
            pl.BlockSpec((1, CONV_CH), const),
            pl.BlockSpec(wo_b.shape, const),
        ],
        out_specs=pl.BlockSpec((T, D), row),
        out_shape=jax.ShapeDtypeStruct((N, D), F32),
        scratch_shapes=[
            pltpu.VMEM((T + 2 * CONV_HALO, CONV_CH), F32),
            pltpu.VMEM((V7X_SUBLANES, T + 2 * CONV_HALO - V7X_SUBLANES, CONV_CH), F32),
        ],
        compiler_params=pltpu.CompilerParams(
            dimension_semantics=("parallel",), vmem_limit_bytes=48 * 1024 * 1024),
    )(hglu, hglu, hglu, attn, x2d, mod3, cw, cb, lg, lb, wo_b)


def _route_kernel(x1_ref, mod_ref, g2_ref, wr_ref, br_ref, xs_ref, rt_ref, ce_ref, *, n_steps):
    i = pl.program_id(0)

    @pl.when(i < n_steps)
    def _():
        _route_tiles(x1_ref, mod_ref, g2_ref, wr_ref, br_ref, xs_ref, rt_ref, ce_ref)

    @pl.when(i >= n_steps)
    def _():
        xs_ref[...] = jnp.zeros(xs_ref.shape, U32)
        rt_ref[...] = jnp.zeros(rt_ref.shape, F32)
        ce_ref[...] = jnp.full(ce_ref.shape, N_EXPERTS, I32)


def _route_tiles(x1_ref, mod_ref, g2_ref, wr_ref, br_ref, xs_ref, rt_ref, ce_ref):
    G, T, E, L = ROUTE_TILES_PER_STEP, TOK_TILE, N_EXPERTS, SORT_ROWS
    W = G * T
    x1 = x1_ref[...]
    ms = jnp.mean(x1 * x1, axis=-1, keepdims=True)
    h2 = ((x1 * lax.rsqrt(ms + EPS)) * g2_ref[...]) * (1.0 + mod_ref[4:5, :]) + mod_ref[3:4, :]
    h2b = h2.astype(BF16)
    logits = lax.dot_general(wr_ref[...], h2, _nt_dims(), precision=HIGHEST,
                             preferred_element_type=F32) + br_ref[...]
    e_iota = lax.broadcasted_iota(I32, (E, W), 0)
    sels, tops = [], []
    l = logits
    for _ in range(TOP_K):
        m = jnp.max(l, axis=0, keepdims=True)
        idx = jnp.min(jnp.where(l == m, e_iota, E), axis=0, keepdims=True)
        sel = e_iota == idx
        l = jnp.where(sel, -jnp.inf, l)
        sels.append(sel)
        tops.append(m)
    ws = [jnp.exp(m - tops[0]) for m in tops]
    den = ws[0] + ws[1] + ws[2] + ws[3]
    gates = [w / den for w in ws]
    multi = jnp.zeros((E, W), F32)
    for sel in sels:
        multi = multi + jnp.where(sel, 1.0, 0.0)
    multi_b = multi.astype(BF16)
    r_iota = lax.broadcasted_iota(I32, (T, T + V7X_LANES), 0)
    c_iota = lax.broadcasted_iota(I32, (T, T + V7X_LANES), 1)
    tri = jnp.where((r_iota < c_iota) | (c_iota >= T), 1.0, 0.0).astype(BF16)
    rks = [jnp.dot(multi_b[:, h * T:(h + 1) * T], tri, preferred_element_type=F32) for h in range(G)]
    cnt = jnp.concatenate([rk[:, T:T + V7X_LANES] for rk in rks], axis=1)
    n8 = jnp.floor((cnt + (CHUNK - 1)) * (1.0 / CHUNK))
    e_iota_l = lax.broadcasted_iota(I32, n8.shape, 0)
    lo8 = jnp.zeros(n8.shape, F32)
    for e in range(E - 1):
        lo8 = lo8 + jnp.where(e_iota_l > e, n8[e:e + 1, :], 0.0)
    s_iota = lax.broadcasted_iota(I32, (E, ce_ref.shape[2]), 1).astype(F32)
    j_iota = lax.broadcasted_iota(I32, (L, T), 0).astype(F32)
    for h in range(G):
        cols = slice(h * T, (h + 1) * T)
        lo_h = lo8[:, h * V7X_LANES:h * V7X_LANES + 1]
        n_h = n8[:, h * V7X_LANES:h * V7X_LANES + 1]
        base = lo_h * float(CHUNK) + rks[h][:, :T]
        dests = [jnp.sum(jnp.where(sel[:, cols], base, 0.0), axis=0, keepdims=True) for sel in sels]
        for k in range(TOP_K):
            rt_ref[h, k:k + 1, :] = dests[k]
            rt_ref[h, TOP_K + k:TOP_K + k + 1, :] = gates[k][:, cols]
        ce_ref[h] = jnp.sum(jnp.where(lo_h + n_h <= s_iota, 1.0, 0.0), axis=0, keepdims=True).astype(I32)
        pm = jnp.zeros((L, T), F32)
        for d in dests:
            pm = pm + jnp.where(j_iota == d, 1.0, 0.0)
        xs_ref[h * L:(h + 1) * L, :] = _pack_halves(
            jnp.dot(pm.astype(BF16), h2b[h * T:(h + 1) * T, :], preferred_element_type=F32), is_bf16_exact=True)


def _route(x1, mod3, g2, wr_t, br, seq, trash_tiles):
    N, D = x1.shape
    T = TOK_TILE
    tiles_per_seq = seq // T
    n_tiles = N // T
    n_ext = n_tiles + trash_tiles
    L = SORT_ROWS
    G = ROUTE_TILES_PER_STEP
    assert tiles_per_seq % G == 0 and n_ext % G == 0
    n_steps = n_tiles // G
    ce_w = _round_up(SORT_CHUNKS, V7X_LANES)
    const = lambda i: (0, 0)
    real = lambda i: jnp.minimum(i, n_steps - 1)
    return pl.pallas_call(
        functools.partial(_route_kernel, n_steps=n_steps),
        name="route",
        grid=(n_ext // G,),
        in_specs=[
            pl.BlockSpec((G * T, D), lambda i: (real(i), 0)),
            pl.BlockSpec((None, N_MOD, D), lambda i: (real(i) // (tiles_per_seq // G), 0, 0)),
            pl.BlockSpec((1, D), const),
            pl.BlockSpec(wr_t.shape, const),
            pl.BlockSpec(br.shape, const),
        ],
        out_specs=[
            pl.BlockSpec((G * L, D // 2), lambda i: (i, 0)),
            pl.BlockSpec((G, 2 * TOP_K, T), lambda i: (i, 0, 0)),
            pl.BlockSpec((G, 1, ce_w), lambda i: (i, 0, 0)),
        ],
        out_shape=[
            jax.ShapeDtypeStruct((n_ext * L, D // 2), U32),
            jax.ShapeDtypeStruct((n_ext, 2 * TOP_K, T), F32),
            jax.ShapeDtypeStruct((n_ext, 1, ce_w), I32),
        ],
        compiler_params=pltpu.CompilerParams(
            dimension_semantics=("parallel",), vmem_limit_bytes=48 * 1024 * 1024),
    )(x1, mod3, g2, wr_t, br)


def _expert_kernel(be_ref, nb_ref, slot_ref, xs_hbm, wgu_ref, bgu_ref, wd_ref, bd_ref, ys_hbm,
                   xbuf, obuf, wgu_b, wd_b, sem_in, sem_out):
    b = pl.program_id(0)
    nb = nb_ref[0]
    dff = wd_ref.shape[0]

    def gather_copy(blk, slot, j):
        chunk = slot_ref[blk * CHUNKS_PER_BLOCK + j]
        return pltpu.make_async_copy(xs_hbm.at[chunk], xbuf.at[slot, pl.ds(j * CHUNK, CHUNK)], sem_in.at[slot])

    def scatter_copy(blk, slot, j):
        chunk = slot_ref[blk * CHUNKS_PER_BLOCK + j]
        return pltpu.make_async_copy(obuf.at[slot, pl.ds(j * CHUNK, CHUNK)], ys_hbm.at[chunk], sem_out.at[slot])

    def start_gather(blk, slot):
        for j in range(CHUNKS_PER_BLOCK):
            gather_copy(blk, slot, j).start()

    def wait_gather(blk, slot):
        for j in range(CHUNKS_PER_BLOCK):
            gather_copy(blk, slot, j).wait()

    def start_scatter(blk, slot):
        for j in range(CHUNKS_PER_BLOCK):
            scatter_copy(blk, slot, j).start()

    def wait_scatter(blk, slot):
        for j in range(CHUNKS_PER_BLOCK):
            scatter_copy(blk, slot, j).wait()

    @pl.when(b < nb)
    def _():
        slot = lax.rem(b, 2)

        @pl.when(b == 0)
        def _():
            start_gather(0, 0)

        @pl.when(b + 1 < nb)
        def _():
            start_gather(b + 1, 1 - slot)

        @pl.when((b == 0) | (be_ref[b] != be_ref[jnp.maximum(b - 1, 0)]))
        def _():
            wgu_b[...] = wgu_ref[...].astype(BF16)
            wd_b[...] = wd_ref[...].astype(BF16)

        wait_gather(b, slot)

        @pl.when(b >= 2)
        def _():
            wait_scatter(b - 2, slot)

        xb = _unpack_halves(xbuf[slot])
        gu = jnp.dot(xb, wgu_b[...], preferred_element_type=F32) + bgu_ref[...]
        g = jnp.minimum(gu[:, :dff], SWIGLU_LIMIT)
        u = jnp.clip(gu[:, dff:], -SWIGLU_LIMIT, SWIGLU_LIMIT)
        y = (u + 1.0) * (g * _sigmoid(SWIGLU_ALPHA * g))
        out = jnp.dot(y.astype(BF16), wd_b[...], preferred_element_type=F32) + bd_ref[...]
        obuf[slot] = _pack_halves(out, is_bf16_exact=False)
        start_scatter(b, slot)

        @pl.when(b == nb - 1)
        def _():
            wait_scatter(b, slot)

            @pl.when(b >= 1)
            def _():
                wait_scatter(b - 1, 1 - slot)


def _experts(blk_expert, nb_total, slots, xs, wgu, bgu, wd, bd):
    nb_max = blk_expert.shape[0]
    half = xs.shape[1]
    D = 2 * half
    f2 = wgu.shape[2]
    dff = wd.shape[1]
    chunks = xs.reshape(xs.shape[0] // CHUNK, CHUNK, half)
    wmap = lambda b, be, nb, sl: (be[b], 0, 0)
    grid_spec = pltpu.PrefetchScalarGridSpec(
        num_scalar_prefetch=3,
        grid=(nb_max,),
        in_specs=[
            pl.BlockSpec(memory_space=pl.ANY),
            pl.BlockSpec((None, D, f2), wmap),
            pl.BlockSpec((None, 1, f2), wmap),
            pl.BlockSpec((None, dff, D), wmap),
            pl.BlockSpec((None, 1, D), wmap),
        ],
        out_specs=pl.BlockSpec(memory_space=pl.ANY),
        scratch_shapes=[
            pltpu.VMEM((2, BLOCK_ROWS, half), U32),
            pltpu.VMEM((2, BLOCK_ROWS, half), U32),
            pltpu.VMEM((D, f2), BF16),
            pltpu.VMEM((dff, D), BF16),
            pltpu.SemaphoreType.DMA((2,)),
            pltpu.SemaphoreType.DMA((2,)),
        ],
    )
    ys = pl.pallas_call(
        _expert_kernel,
        name="experts",
        grid_spec=grid_spec,
        out_shape=jax.ShapeDtypeStruct(chunks.shape, U32),
        input_output_aliases={3: 0},
        compiler_params=pltpu.CompilerParams(
            dimension_semantics=("arbitrary",), vmem_limit_bytes=56 * 1024 * 1024),
    )(blk_expert, nb_total, slots, chunks, wgu, bgu, wd, bd)
    return ys.reshape(xs.shape)


def _combine_kernel(ys_ref, rt_ref, x1_ref, mod_ref, x2_ref):
    T, L = TOK_TILE, SORT_ROWS
    j_iota = lax.broadcasted_iota(I32, (T, L), 1).astype(F32)
    for h in range(COMBINE_TILES_PER_STEP):
        rows = slice(h * T, (h + 1) * T)
        wm = jnp.zeros((T, L), F32)
        for k in range(TOP_K):
            wm = wm + jnp.where(j_iota == rt_ref[rows, k:k + 1], rt_ref[rows, TOP_K + k:TOP_K + k + 1], 0.0)
        y = jnp.dot(wm.astype(BF16), _unpack_halves(ys_ref[h * L:(h + 1) * L, :]), preferred_element_type=F32)
        x2_ref[rows, :] = x1_ref[rows, :] + mod_ref[5:6, :] * y


def _combine(ys, rt, x1, mod3, seq):
    N, D = x1.shape
    G = COMBINE_TILES_PER_STEP
    T = G * TOK_TILE
    tiles_per_seq = seq // T
    L = G * SORT_ROWS
    assert seq % T == 0
    return pl.pallas_call(
        _combine_kernel,
        name="combine",
        grid=(N // T,),
        in_specs=[
            pl.BlockSpec((L, D // 2), lambda i: (i, 0)),
            pl.BlockSpec((T, 2 * TOP_K), lambda i: (i, 0)),
            pl.BlockSpec((T, D), lambda i: (i, 0)),
            pl.BlockSpec((None, N_MOD, D), lambda i: (i // tiles_per_seq, 0, 0)),
        ],
        out_specs=pl.BlockSpec((T, D), lambda i: (i, 0)),
        out_shape=jax.ShapeDtypeStruct((N, D), F32),
        compiler_params=pltpu.CompilerParams(
            dimension_semantics=("parallel",), vmem_limit_bytes=48 * 1024 * 1024),
    )(ys, rt, x1, mod3)


def _block_tables(chunk_expert, n_tiles):
    E, cpb = N_EXPERTS, CHUNKS_PER_BLOCK
    keys = chunk_expert[:, 0, :SORT_CHUNKS].reshape(-1)
    order = jnp.argsort(keys, stable=True).astype(I32)
    cc = jnp.sum((keys[:, None] == jnp.arange(E, dtype=I32)[None, :]).astype(I32), axis=0)
    cend = jnp.cumsum(cc)
    cstart = cend - cc
    nbk = (cc + cpb - 1) // cpb
    bend = jnp.cumsum(nbk)
    bstart = bend - nbk
    nb_total = bend[-1:]
    nb_max = -(-(n_tiles * USED_CHUNKS_MAX) // cpb) + E
    bidx = jnp.arange(nb_max, dtype=I32)
    be = jnp.minimum(jnp.sum((bidx[:, None] >= bend[None, :]).astype(I32), axis=1), E - 1)
    j = jnp.arange(cpb, dtype=I32)[None, :]
    pos = cstart[be][:, None] + (bidx - bstart[be])[:, None] * cpb + j
    valid = (pos < cend[be][:, None]) & (bidx < nb_total[0])[:, None]
    src = order[jnp.clip(pos, 0, order.shape[0] - 1)]
    trash = be[:, None] * cpb + j
    slots = jnp.where(valid, src, n_tiles * SORT_CHUNKS + trash).astype(I32)
    return be, nb_total.astype(I32), slots.reshape(-1)


def kernel(x, c, w_ada, b_ada, norm1_g, w_in, q_norm_g, k_norm_g, lambda_q1, lambda_k1, lambda_q2,
           lambda_k2, subln_g, conv_w, conv_b, conv_ln_g, conv_ln_b, w_out, norm2_g, w_router, b_router,
           w_gate_up, b_gate_up, w_down, b_down):
    B, S, D = x.shape
    N = B * S
    assert S % INPROJ_ROWS == 0 and S % ATTN_SUB_ROWS == 0 and S % TOK_TILE == 0
    n_tiles = N // TOK_TILE
    trash_tiles = _round_up(-(-(N_EXPERTS * CHUNKS_PER_BLOCK) // SORT_CHUNKS), ROUTE_TILES_PER_STEP)
    log2e = math.log2(math.e)
    slopes = jnp.exp2(-8.0 * jnp.arange(1, ATTN_HEADS + 1, dtype=F32) / ATTN_HEADS) * log2e
    qfeat, kfeat = _alibi_features(slopes, S)
    xt = x.reshape(N, D)
    for l in range(w_ada.shape[0]):
        lam_init = 0.8 - 0.6 * math.exp(-0.3 * l)
        row = lambda a: a[l][None, :]
        mod, lam = _ada(c, w_ada[l], row(b_ada), row(lambda_q1), row(lambda_k1), row(lambda_q2),
                        row(lambda_k2), lam_init)
        mod3 = mod.reshape(B, N_MOD, D)
        qg = jnp.tile(q_norm_g[l], 2 * ATTN_HEADS)[None, :] * (ATTN_HEAD_DIM ** -0.5 * log2e)
        kg = jnp.tile(k_norm_g[l], 2 * ATTN_HEADS)[None, :]
        hglu, q, k, v = _inproj(xt, mod3, row(norm1_g), w_in[l].astype(BF16), qg, kg, S)
        attn = _attention(lam[0, 0:1], slopes, qfeat, kfeat, q, k, v, row(subln_g), B, S, lam_init)
        cw = jnp.pad(conv_w[l], ((0, 1), (0, 0)))
        x1 = _mixout(hglu, attn, xt, mod3, cw, row(conv_b), row(conv_ln_g), row(conv_ln_b),
                     w_out[l].astype(BF16), S)
        xs, rt, ce = _route(x1, mod3, row(norm2_g), w_router[l].T, b_router[l][:, None], S, trash_tiles)
        be, nb_total, slots = _block_tables(ce[:n_tiles], n_tiles)
        ys = _experts(be, nb_total, slots, xs, w_gate_up[l], b_gate_up[l][:, None, :],
                      w_down[l], b_down[l][:, None, :])
        rt_tok = rt[:n_tiles].transpose(0, 2, 1).reshape(N, 2 * TOP_K)
        xt = _combine(ys, rt_tok, x1, mod3, S)
    return xt.reshape(B, S, D)
```

```python
import functools
import math

import jax
import jax.numpy as jnp
import numpy as np
from jax import lax
from jax.experimental import pallas as pl
from jax.experimental.pallas import tpu as pltpu

F32 = jnp.float32
BF16 = jnp.bfloat16
I32 = jnp.int32
U32 = jnp.uint32
HIGHEST = lax.Precision.HIGHEST

CONV_CH = 512
CONV_KERNEL = 31
CONV_HALO = 16
ATTN_HEADS = 4
ATTN_HEAD_DIM = 64
ATTN_V_DIM = 2 * ATTN_HEAD_DIM
ATTN_WIDTH = ATTN_HEADS * ATTN_V_DIM
N_EXPERTS = 32
TOP_K = 4
SWIGLU_LIMIT = 7.0
SWIGLU_ALPHA = 1.702
N_MOD = 6
EPS = 1e-5

V7X_LANES = 128
V7X_SUBLANES = 8

INPROJ_ROWS = 1024
ATTN_SUB_ROWS = 256
TOK_TILE = 256
COMBINE_TILES_PER_STEP = 2
ROUTE_TILES_PER_STEP = 4
CHUNK = V7X_SUBLANES
BLOCK_ROWS = 512
CHUNKS_PER_BLOCK = BLOCK_ROWS // CHUNK


def _round_up(a, b):
    return (a + b - 1) // b * b


USED_CHUNKS_MAX = (TOP_K * TOK_TILE + N_EXPERTS * (CHUNK - 1)) // CHUNK
SORT_ROWS = _round_up(USED_CHUNKS_MAX * CHUNK, V7X_LANES)
SORT_CHUNKS = SORT_ROWS // CHUNK


def _sigmoid(v):
    return 1.0 / (1.0 + jnp.exp(-v))


def _nt_dims():
    return (((1,), (1,)), ((), ()))


def _pack_halves(v, *, is_bf16_exact):
    c = v.shape[1] // 2
    bits = lax.bitcast_convert_type(v, U32)
    if not is_bf16_exact:
        bits = bits + jnp.uint32(0x7FFF) + ((bits >> 16) & jnp.uint32(1))
    return (bits[:, :c] >> 16) | (bits[:, c:] & jnp.uint32(0xFFFF0000))


def _unpack_halves(u):
    lo = lax.bitcast_convert_type(u << 16, F32).astype(BF16)
    hi = lax.bitcast_convert_type(u & jnp.uint32(0xFFFF0000), F32).astype(BF16)
    return jnp.concatenate([lo, hi], axis=1)


def _ada_kernel(c_ref, w_ref, b_ref, lq1_ref, lk1_ref, lq2_ref, lk2_ref, mod_ref, lam_ref, *, lam_init):
    c = c_ref[...]
    sc = c * _sigmoid(c)
    mod_ref[...] = jnp.dot(sc, w_ref[...], precision=HIGHEST, preferred_element_type=F32) + b_ref[...]
    s1 = jnp.sum(lq1_ref[...] * lk1_ref[...], axis=-1, keepdims=True)
    s2 = jnp.sum(lq2_ref[...] * lk2_ref[...], axis=-1, keepdims=True)
    lam = jnp.exp(s1) - jnp.exp(s2) + lam_init
    lam_ref[...] = jnp.broadcast_to(lam, lam_ref.shape)


def _ada(c, w_ada, b_ada, lq1, lk1, lq2, lk2, lam_init):
    B, D = c.shape
    cols = w_ada.shape[1]
    bc = D
    vec = pl.BlockSpec((1, ATTN_HEAD_DIM), lambda j: (0, 0))
    return pl.pallas_call(
        functools.partial(_ada_kernel, lam_init=lam_init),
        name="ada",
        grid=(cols // bc,),
        in_specs=[
            pl.BlockSpec((B, D), lambda j: (0, 0)),
            pl.BlockSpec((D, bc), lambda j: (0, j)),
            pl.BlockSpec((1, bc), lambda j: (0, j)),
            vec, vec, vec, vec,
        ],
        out_specs=[
            pl.BlockSpec((B, bc), lambda j: (0, j)),
            pl.BlockSpec((V7X_SUBLANES, V7X_LANES), lambda j: (0, 0)),
        ],
        out_shape=[
            jax.ShapeDtypeStruct((B, cols), F32),
            jax.ShapeDtypeStruct((V7X_SUBLANES, V7X_LANES), F32),
        ],
        compiler_params=pltpu.CompilerParams(dimension_semantics=("arbitrary",)),
    )(c, w_ada, b_ada, lq1, lk1, lq2, lk2)


def _group_rms_scale(t, lo):
    sq = t * t
    s1 = jnp.sum(jnp.where(lo, sq, 0.0), axis=-1, keepdims=True)
    s2 = jnp.sum(jnp.where(lo, 0.0, sq), axis=-1, keepdims=True)
    r1 = lax.rsqrt(s1 * (1.0 / ATTN_HEAD_DIM) + EPS)
    r2 = lax.rsqrt(s2 * (1.0 / ATTN_HEAD_DIM) + EPS)
    return jnp.where(lo, r1, r2)


def _inproj_kernel(x_ref, mod_ref, g1_ref, w_ref, qg_ref, kg_ref, hglu_ref, q_ref, k_ref, v_ref):
    x = x_ref[...]
    ms = jnp.mean(x * x, axis=-1, keepdims=True)
    shift = mod_ref[0:1, :]
    scale = mod_ref[1:2, :]
    h = ((x * lax.rsqrt(ms + EPS)) * g1_ref[...]) * (1.0 + scale) + shift
    hb = h.astype(BF16)
    c2 = 2 * CONV_CH
    ag = jnp.dot(hb, w_ref[:, 0:c2], preferred_element_type=F32)
    hglu_ref[...] = ag[:, :CONV_CH] * _sigmoid(ag[:, CONV_CH:])
    lo = lax.broadcasted_iota(I32, (1, V7X_LANES), 1) < ATTN_HEAD_DIM
    for src_col, g_ref, o_ref in ((c2, qg_ref, q_ref), (c2 + ATTN_WIDTH, kg_ref, k_ref)):
        t = jnp.dot(hb, w_ref[:, src_col:src_col + ATTN_WIDTH], preferred_element_type=F32)
        for hd in range(ATTN_HEADS):
            sl = slice(hd * V7X_LANES, (hd + 1) * V7X_LANES)
            th = t[:, sl]
            o_ref[:, sl] = (th * _group_rms_scale(th, lo) * g_ref[:, sl]).astype(BF16)
    v0 = c2 + 2 * ATTN_WIDTH
    v_ref[...] = jnp.dot(hb, w_ref[:, v0:v0 + ATTN_WIDTH], preferred_element_type=F32).astype(BF16)


def _inproj(x2d, mod3, g1, w_in_b, qg, kg, seq):
    N, D = x2d.shape
    tm = INPROJ_ROWS
    steps_per_seq = seq // tm
    cols = w_in_b.shape[1]
    row = lambda i: (i, 0)
    const = lambda i: (0, 0)
    return pl.pallas_call(
        _inproj_kernel,
        name="inproj",
        grid=(N // tm,),
        in_specs=[
            pl.BlockSpec((tm, D), row),
            pl.BlockSpec((None, N_MOD, D), lambda i: (i // steps_per_seq, 0, 0)),
            pl.BlockSpec((1, D), const),
            pl.BlockSpec((D, cols), const),
            pl.BlockSpec((1, ATTN_WIDTH), const),
            pl.BlockSpec((1, ATTN_WIDTH), const),
        ],
        out_specs=[
            pl.BlockSpec((tm, CONV_CH), row),
            pl.BlockSpec((tm, ATTN_WIDTH), row),
            pl.BlockSpec((tm, ATTN_WIDTH), row),
            pl.BlockSpec((tm, ATTN_WIDTH), row),
        ],
        out_shape=[
            jax.ShapeDtypeStruct((N, CONV_CH), F32),
            jax.ShapeDtypeStruct((N, ATTN_WIDTH), BF16),
            jax.ShapeDtypeStruct((N, ATTN_WIDTH), BF16),
            jax.ShapeDtypeStruct((N, ATTN_WIDTH), BF16),
        ],
        compiler_params=pltpu.CompilerParams(
            dimension_semantics=("parallel",), vmem_limit_bytes=48 * 1024 * 1024),
    )(x2d, mod3, g1, w_in_b, qg, kg)


ATTN_FEAT = 9


def _split3_bf16(x):
    def trunc(v):
        return (np.asarray(v, np.float32).view(np.uint32) & np.uint32(0xFFFF0000)).view(np.float32)
    x = np.asarray(x, np.float32)
    x0 = trunc(x)
    x1 = trunc(x - x0)
    x2 = trunc(x - x0 - x1)
    return [x0, x1, x2]


def _alibi_constants(seq):
    slopes = (np.exp2(-8.0 * np.arange(1, ATTN_HEADS + 1) / ATTN_HEADS) * math.log2(math.e)).astype(np.float32)
    pos = np.arange(seq)
    a = (pos // 16).astype(np.float32)
    b = (pos % 16).astype(np.float32)
    one = np.ones((seq,), np.float32)
    qfeat = np.zeros((seq, V7X_LANES), np.float32)
    qfeat[:, :ATTN_FEAT] = np.stack([a, a, a, b, b, b, one, one, one], axis=1)
    kfeat = np.zeros((ATTN_HEADS, seq, V7X_LANES), np.float32)
    for h in range(ATTN_HEADS):
        sl = slopes[h]
        cols = ([-c * one for c in _split3_bf16(np.float32(16.0) * sl)] + [-c * one for c in _split3_bf16(sl)]
                + _split3_bf16(sl * pos.astype(np.float32)))
        kfeat[h, :, :ATTN_FEAT] = np.stack(cols, axis=1)
    return jnp.asarray(slopes), jnp.asarray(qfeat, BF16), jnp.asarray(kfeat, BF16)


def _attn_kernel(lam_ref, slope_ref, q_ref, qf_ref, k_ref, kf_ref, v_ref, sg_ref, o_ref, *, ts, lam_init):
    hd = pl.program_id(1)
    lam = lam_ref[0]
    slope = slope_ref[hd]
    S = k_ref.shape[0]
    lo = lax.broadcasted_iota(I32, (1, V7X_LANES), 1) < ATTN_HEAD_DIM
    rel = lax.broadcasted_iota(I32, (ts, ts), 0) - lax.broadcasted_iota(I32, (ts, ts), 1)
    dbias = jnp.abs(rel).astype(F32) * slope
    k = k_ref[...]
    kf = kf_ref[...]
    v1 = jnp.concatenate([v_ref[...], jnp.ones((S, ATTN_V_DIM), BF16)], axis=1)
    k_left = jnp.concatenate([k, kf], axis=1)
    k_right = jnp.concatenate([k, -kf], axis=1)
    dn = _nt_dims()

    for t in range(S // ts):
        rows = slice(t * ts, (t + 1) * ts)
        q = q_ref[rows, :]
        qf = qf_ref[rows, :]
        zero = jnp.zeros_like(q)
        outs = []
        for qc in (jnp.where(lo, q, zero), jnp.where(lo, zero, q)):
            qa = jnp.concatenate([qc, qf], axis=1)
            parts = [(lax.dot_general(qc, k_ref[rows, :], dn, preferred_element_type=F32) - dbias, v1[rows])]
            if t > 0:
                parts.append((lax.dot_general(qa, k_left[:t * ts], dn, preferred_element_type=F32),
                              v1[:t * ts]))
            if (t + 1) * ts < S:
                parts.append((lax.dot_general(qa, k_right[(t + 1) * ts:], dn, preferred_element_type=F32),
                              v1[(t + 1) * ts:]))
            m = parts[0][0].max(axis=-1, keepdims=True)
            for sp, _ in parts[1:]:
                m = jnp.maximum(m, sp.max(axis=-1, keepdims=True))
            acc = jnp.zeros((ts, 2 * ATTN_V_DIM), F32)
            for sp, vp in parts:
                acc = acc + jnp.dot(jnp.exp2(sp - m).astype(BF16), vp, preferred_element_type=F32)
            outs.append((acc[:, :ATTN_V_DIM], acc[:, ATTN_V_DIM:ATTN_V_DIM + 1]))
        (a1, l1), (a2, l2) = outs
        o = a1 * (1.0 / l1) - a2 * (lam / l2)
        ms = jnp.mean(o * o, axis=-1, keepdims=True)
        o = (o * lax.rsqrt(ms + EPS)) * sg_ref[...] * (1.0 - lam_init)
        o_ref[rows, :] = o.astype(BF16)


def _attention(lam1, slopes, qfeat, kfeat, q, k, v, sg, batch, seq, lam_init):
    smem = pl.BlockSpec(memory_space=pltpu.SMEM)
    seq_blk = lambda b, h: (b, h)
    return pl.pallas_call(
        functools.partial(_attn_kernel, ts=ATTN_SUB_ROWS, lam_init=lam_init),
        name="attn",
        grid=(batch, ATTN_HEADS),
        in_specs=[
            smem, smem,
            pl.BlockSpec((seq, ATTN_V_DIM), seq_blk),
            pl.BlockSpec((seq, V7X_LANES), lambda b, h: (0, 0)),
            pl.BlockSpec((seq, ATTN_V_DIM), seq_blk),
            pl.BlockSpec((None, seq, V7X_LANES), lambda b, h: (h, 0, 0)),
            pl.BlockSpec((seq, ATTN_V_DIM), seq_blk),
            pl.BlockSpec((1, ATTN_V_DIM), lambda b, h: (0, 0)),
        ],
        out_specs=pl.BlockSpec((seq, ATTN_V_DIM), seq_blk),
        out_shape=jax.ShapeDtypeStruct(q.shape, BF16),
        compiler_params=pltpu.CompilerParams(
            dimension_semantics=("parallel", "parallel"), vmem_limit_bytes=48 * 1024 * 1024),
    )(lam1, slopes, q, qfeat, k, kfeat, v, sg)


def _mixout_kernel(prev_ref, main_ref, next_ref, attn_ref, x_ref, mod_ref, cw_ref, cb_ref, lg_ref, lb_ref,
                   wo_ref, x1_ref, win_ref, sh_ref, *, tiles_per_seq):
    T = main_ref.shape[0]
    j = lax.rem(pl.program_id(0), tiles_per_seq)
    H = CONV_HALO
    win_ref[0:H, :] = jnp.where(j == 0, 0.0, prev_ref[...])
    win_ref[H:H + T, :] = main_ref[...]
    win_ref[H + T:H + T + H, :] = jnp.where(j == tiles_per_seq - 1, 0.0, next_ref[...])
    rows = sh_ref.shape[1]
    for r in range(1, V7X_SUBLANES):
        sh_ref[r] = win_ref[r:r + rows, :]
    acc = jnp.broadcast_to(cb_ref[...], (T, CONV_CH))
    off = H - CONV_KERNEL // 2
    for tap in range(CONV_KERNEL):
        a, r = divmod(off + tap, V7X_SUBLANES)
        src = win_ref if r == 0 else sh_ref.at[r]
        acc = acc + cw_ref[tap:tap + 1, :] * src[a * V7X_SUBLANES:a * V7X_SUBLANES + T, :]
    mu = jnp.mean(acc, axis=-1, keepdims=True)
    xc = acc - mu
    var = jnp.mean(xc * xc, axis=-1, keepdims=True)
    y = (xc * lax.rsqrt(var + EPS)) * lg_ref[...] + lb_ref[...]
    y = y * _sigmoid(y)
    mix = jnp.dot(y.astype(BF16), wo_ref[0:CONV_CH, :], preferred_element_type=F32)
    mix = mix + jnp.dot(attn_ref[...], wo_ref[CONV_CH:CONV_CH + ATTN_WIDTH, :], preferred_element_type=F32)
    x1_ref[...] = x_ref[...] + mod_ref[2:3, :] * mix


def _mixout(hglu, attn, x2d, mod3, cw, cb, lg, lb, wo_b, seq):
    N, D = x2d.shape
    T = TOK_TILE
    tiles_per_seq = seq // T
    hpt = T // CONV_HALO
    n_halo = N // CONV_HALO
    row = lambda i: (i, 0)
    const = lambda i: (0, 0)
    return pl.pallas_call(
        functools.partial(_mixout_kernel, tiles_per_seq=tiles_per_seq),
        name="mixout",
        grid=(N // T,),
        in_specs=[
            pl.BlockSpec((CONV_HALO, CONV_CH), lambda i: (jnp.maximum(i * hpt - 1, 0), 0)),
            pl.BlockSpec((T, CONV_CH), row),
            pl.BlockSpec((CONV_HALO, CONV_CH), lambda i: (jnp.minimum((i + 1) * hpt, n_halo - 1), 0)),
            pl.BlockSpec((T, ATTN_WIDTH), row),
            pl.BlockSpec((T, D), row),
            pl.BlockSpec((None, N_MOD, D), lambda i: (i // tiles_per_seq, 0, 0)),
            pl.BlockSpec(cw.shape, const),
            pl.BlockSpec((1, CONV_CH), const),
            pl.BlockSpec((1, CONV_CH), const),
            pl.BlockSpec((1, CONV_CH), const),
            pl.BlockSpec(wo_b.shape, const),
        ],
        out_specs=pl.BlockSpec((T, D), row),
        out_shape=jax.ShapeDtypeStruct((N, D), F32),
        scratch_shapes=[
            pltpu.VMEM((T + 2 * CONV_HALO, CONV_CH), F32),
            pltpu.VMEM((V7X_SUBLANES, T + 2 * CONV_HALO - V7X_SUBLANES, CONV_CH), F32),
        ],
        compiler_params=pltpu.CompilerParams(
            dimension_semantics=("parallel",), vmem_limit_bytes=48 * 1024 * 1024),
    )(hglu, hglu, hglu, attn, x2d, mod3, cw, cb, lg, lb, wo_b)


def _route_kernel(x1_ref, mod_ref, g2_ref, wr_ref, br_ref, xs_ref, rt_ref, ce_ref, *, n_steps):
    i = pl.program_id(0)

    @pl.when(i < n_steps)
    def _():
        _route_tiles(x1_ref, mod_ref, g2_ref, wr_ref, br_ref, xs_ref, rt_ref, ce_ref)

    @pl.when(i >= n_steps)
    def _():
        xs_ref[...] = jnp.zeros(xs_ref.shape, U32)
        rt_ref[...] = jnp.zeros(rt_ref.shape, F32)
        ce_ref[...] = jnp.full(ce_ref.shape, N_EXPERTS, I32)


def _route_tiles(x1_ref, mod_ref, g2_ref, wr_ref, br_ref, xs_ref, rt_ref, ce_ref):
    G, T, E, L = ROUTE_TILES_PER_STEP, TOK_TILE, N_EXPERTS, SORT_ROWS
    W = G * T
    x1 = x1_ref[...]
    ms = jnp.mean(x1 * x1, axis=-1, keepdims=True)
    h2 = ((x1 * lax.rsqrt(ms + EPS)) * g2_ref[...]) * (1.0 + mod_ref[4:5, :]) + mod_ref[3:4, :]
    h2b = h2.astype(BF16)
    logits = lax.dot_general(wr_ref[...], h2, _nt_dims(), precision=HIGHEST,
                             preferred_element_type=F32) + br_ref[...]
    e_iota = lax.broadcasted_iota(I32, (E, W), 0)
    sels, tops = [], []
    l = logits
    for _ in range(TOP_K):
        m = jnp.max(l, axis=0, keepdims=True)
        idx = jnp.min(jnp.where(l == m, e_iota, E), axis=0, keepdims=True)
        sel = e_iota == idx
        l = jnp.where(sel, -jnp.inf, l)
        sels.append(sel)
        tops.append(m)
    ws = [jnp.exp(m - tops[0]) for m in tops]
    den = ws[0] + ws[1] + ws[2] + ws[3]
    gates = [w / den for w in ws]
    multi = jnp.zeros((E, W), F32)
    for sel in sels:
        multi = multi + jnp.where(sel, 1.0, 0.0)
    multi_b = multi.astype(BF16)
    r_iota = lax.broadcasted_iota(I32, (T, T + V7X_LANES), 0)
    c_iota = lax.broadcasted_iota(I32, (T, T + V7X_LANES), 1)
    tri = jnp.where((r_iota < c_iota) | (c_iota >= T), 1.0, 0.0).astype(BF16)
    rks = [jnp.dot(multi_b[:, h * T:(h + 1) * T], tri, preferred_element_type=F32) for h in range(G)]
    cnt = jnp.concatenate([rk[:, T:T + V7X_LANES] for rk in rks], axis=1)
    n8 = jnp.floor((cnt + (CHUNK - 1)) * (1.0 / CHUNK))
    e_iota_l = lax.broadcasted_iota(I32, n8.shape, 0)
    lo8 = jnp.zeros(n8.shape, F32)
    for e in range(E - 1):
        lo8 = lo8 + jnp.where(e_iota_l > e, n8[e:e + 1, :], 0.0)
    s_iota = lax.broadcasted_iota(I32, (E, ce_ref.shape[2]), 1).astype(F32)
    j_iota = lax.broadcasted_iota(I32, (L, T), 0).astype(F32)
    for h in range(G):
        cols = slice(h * T, (h + 1) * T)
        lo_h = lo8[:, h * V7X_LANES:h * V7X_LANES + 1]
        n_h = n8[:, h * V7X_LANES:h * V7X_LANES + 1]
        base = lo_h * float(CHUNK) + rks[h][:, :T]
        dests = [jnp.sum(jnp.where(sel[:, cols], base, 0.0), axis=0, keepdims=True) for sel in sels]
        for k in range(TOP_K):
            rt_ref[h, k:k + 1, :] = dests[k]
            rt_ref[h, TOP_K + k:TOP_K + k + 1, :] = gates[k][:, cols]
        ce_ref[h] = jnp.sum(jnp.where(lo_h + n_h <= s_iota, 1.0, 0.0), axis=0, keepdims=True).astype(I32)
        pm = jnp.zeros((L, T), F32)
        for d in dests:
            pm = pm + jnp.where(j_iota == d, 1.0, 0.0)
        xs_ref[h * L:(h + 1) * L, :] = _pack_halves(
            jnp.dot(pm.astype(BF16), h2b[h * T:(h + 1) * T, :], preferred_element_type=F32), is_bf16_exact=True)


def _route(x1, mod3, g2, wr_t, br, seq, trash_tiles):
    N, D = x1.shape
    T = TOK_TILE
    tiles_per_seq = seq // T
    n_tiles = N // T
    n_ext = n_tiles + trash_tiles
    L = SORT_ROWS
    G = ROUTE_TILES_PER_STEP
    assert tiles_per_seq % G == 0 and n_ext % G == 0
    n_steps = n_tiles // G
    ce_w = _round_up(SORT_CHUNKS, V7X_LANES)
    const = lambda i: (0, 0)
    real = lambda i: jnp.minimum(i, n_steps - 1)
    return pl.pallas_call(
        functools.partial(_route_kernel, n_steps=n_steps),
        name="route",
        grid=(n_ext // G,),
        in_specs=[
            pl.BlockSpec((G * T, D), lambda i: (real(i), 0)),
            pl.BlockSpec((None, N_MOD, D), lambda i: (real(i) // (tiles_per_seq // G), 0, 0)),
            pl.BlockSpec((1, D), const),
            pl.BlockSpec(wr_t.shape, const),
            pl.BlockSpec(br.shape, const),
        ],
        out_specs=[
            pl.BlockSpec((G * L, D // 2), lambda i: (i, 0)),
            pl.BlockSpec((G, 2 * TOP_K, T), lambda i: (i, 0, 0)),
            pl.BlockSpec((G, 1, ce_w), lambda i: (i, 0, 0)),
        ],
        out_shape=[
            jax.ShapeDtypeStruct((n_ext * L, D // 2), U32),
            jax.ShapeDtypeStruct((n_ext, 2 * TOP_K, T), F32),
            jax.ShapeDtypeStruct((n_ext, 1, ce_w), I32),
        ],
        compiler_params=pltpu.CompilerParams(
            dimension_semantics=("parallel",), vmem_limit_bytes=48 * 1024 * 1024),
    )(x1, mod3, g2, wr_t, br)


def _expert_kernel(be_ref, nb_ref, slot_ref, xs_hbm, wgu_ref, bgu_ref, wd_ref, bd_ref, ys_hbm,
                   xbuf, obuf, wgu_b, wd_b, sem_in, sem_out):
    b = pl.program_id(0)
    nb = nb_ref[0]
    dff = wd_ref.shape[0]

    def gather_copy(blk, slot, j):
        chunk = slot_ref[blk * CHUNKS_PER_BLOCK + j]
        return pltpu.make_async_copy(xs_hbm.at[chunk], xbuf.at[slot, pl.ds(j * CHUNK, CHUNK)], sem_in.at[slot])

    def scatter_copy(blk, slot, j):
        chunk = slot_ref[blk * CHUNKS_PER_BLOCK + j]
        return pltpu.make_async_copy(obuf.at[slot, pl.ds(j * CHUNK, CHUNK)], ys_hbm.at[chunk], sem_out.at[slot])

    def start_gather(blk, slot):
        for j in range(CHUNKS_PER_BLOCK):
            gather_copy(blk, slot, j).start()

    def wait_gather(blk, slot):
        for j in range(CHUNKS_PER_BLOCK):
            gather_copy(blk, slot, j).wait()

    def start_scatter(blk, slot):
        for j in range(CHUNKS_PER_BLOCK):
            scatter_copy(blk, slot, j).start()

    def wait_scatter(blk, slot):
        for j in range(CHUNKS_PER_BLOCK):
            scatter_copy(blk, slot, j).wait()

    @pl.when(b < nb)
    def _():
        slot = lax.rem(b, 2)

        @pl.when(b == 0)
        def _():
            start_gather(0, 0)

        @pl.when(b + 1 < nb)
        def _():
            start_gather(b + 1, 1 - slot)

        @pl.when((b == 0) | (be_ref[b] != be_ref[jnp.maximum(b - 1, 0)]))
        def _():
            wgu_b[...] = wgu_ref[...].astype(BF16)
            wd_b[...] = wd_ref[...].astype(BF16)

        wait_gather(b, slot)

        @pl.when(b >= 2)
        def _():
            wait_scatter(b - 2, slot)

        xb = _unpack_halves(xbuf[slot])
        gu = jnp.dot(xb, wgu_b[...], preferred_element_type=F32) + bgu_ref[...]
        g = jnp.minimum(gu[:, :dff], SWIGLU_LIMIT)
        u = jnp.clip(gu[:, dff:], -SWIGLU_LIMIT, SWIGLU_LIMIT)
        y = (u + 1.0) * (g * _sigmoid(SWIGLU_ALPHA * g))
        out = jnp.dot(y.astype(BF16), wd_b[...], preferred_element_type=F32) + bd_ref[...]
        obuf[slot] = _pack_halves(out, is_bf16_exact=False)
        start_scatter(b, slot)

        @pl.when(b == nb - 1)
        def _():
            wait_scatter(b, slot)

            @pl.when(b >= 1)
            def _():
                wait_scatter(b - 1, 1 - slot)


def _experts(blk_expert, nb_total, slots, xs, wgu, bgu, wd, bd):
    nb_max = blk_expert.shape[0]
    half = xs.shape[1]
    D = 2 * half
    f2 = wgu.shape[2]
    dff = wd.shape[1]
    chunks = xs.reshape(xs.shape[0] // CHUNK, CHUNK, half)
    wmap = lambda b, be, nb, sl: (be[b], 0, 0)
    grid_spec = pltpu.PrefetchScalarGridSpec(
        num_scalar_prefetch=3,
        grid=(nb_max,),
        in_specs=[
            pl.BlockSpec(memory_space=pl.ANY),
            pl.BlockSpec((None, D, f2), wmap),
            pl.BlockSpec((None, 1, f2), wmap),
            pl.BlockSpec((None, dff, D), wmap),
            pl.BlockSpec((None, 1, D), wmap),
        ],
        out_specs=pl.BlockSpec(memory_space=pl.ANY),
        scratch_shapes=[
            pltpu.VMEM((2, BLOCK_ROWS, half), U32),
            pltpu.VMEM((2, BLOCK_ROWS, half), U32),
            pltpu.VMEM((D, f2), BF16),
            pltpu.VMEM((dff, D), BF16),
            pltpu.SemaphoreType.DMA((2,)),
            pltpu.SemaphoreType.DMA((2,)),
        ],
    )
    ys = pl.pallas_call(
        _expert_kernel,
        name="experts",
        grid_spec=grid_spec,
        out_shape=jax.ShapeDtypeStruct(chunks.shape, U32),
        input_output_aliases={3: 0},
        compiler_params=pltpu.CompilerParams(
            dimension_semantics=("arbitrary",), vmem_limit_bytes=56 * 1024 * 1024),
    )(blk_expert, nb_total, slots, chunks, wgu, bgu, wd, bd)
    return ys.reshape(xs.shape)


def _combine_kernel(ys_ref, rt_ref, x1_ref, mod_ref, x2_ref):
    T, L = TOK_TILE, SORT_ROWS
    j_iota = lax.broadcasted_iota(I32, (T, L), 1).astype(F32)
    for h in range(COMBINE_TILES_PER_STEP):
        rows = slice(h * T, (h + 1) * T)
        wm = jnp.zeros((T, L), F32)
        for k in range(TOP_K):
            wm = wm + jnp.where(j_iota == rt_ref[rows, k:k + 1], rt_ref[rows, TOP_K + k:TOP_K + k + 1], 0.0)
        y = jnp.dot(wm.astype(BF16), _unpack_halves(ys_ref[h * L:(h + 1) * L, :]), preferred_element_type=F32)
        x2_ref[rows, :] = x1_ref[rows, :] + mod_ref[5:6, :] * y


def _combine(ys, rt, x1, mod3, seq):
    N, D = x1.shape
    G = COMBINE_TILES_PER_STEP
    T = G * TOK_TILE
    tiles_per_seq = seq // T
    L = G * SORT_ROWS
    assert seq % T == 0
    return pl.pallas_call(
        _combine_kernel,
        name="combine",
        grid=(N // T,),
        in_specs=[
            pl.BlockSpec((L, D // 2), lambda i: (i, 0)),
            pl.BlockSpec((T, 2 * TOP_K), lambda i: (i, 0)),
            pl.BlockSpec((T, D), lambda i: (i, 0)),
            pl.BlockSpec((None, N_MOD, D), lambda i: (i // tiles_per_seq, 0, 0)),
        ],
        out_specs=pl.BlockSpec((T, D), lambda i: (i, 0)),
        out_shape=jax.ShapeDtypeStruct((N, D), F32),
        compiler_params=pltpu.CompilerParams(
            dimension_semantics=("parallel",), vmem_limit_bytes=48 * 1024 * 1024),
    )(ys, rt, x1, mod3)


def _block_tables(chunk_expert, n_tiles):
    E, cpb = N_EXPERTS, CHUNKS_PER_BLOCK
    e_ids = jnp.arange(E, dtype=I32)
    n8 = jnp.sum((chunk_expert[:, 0, :SORT_CHUNKS, None] == e_ids).astype(I32), axis=1)
    lo8 = jnp.cumsum(n8, axis=1) - n8
    excl = jnp.cumsum(n8, axis=0) - n8
    cc = jnp.sum(n8, axis=0)
    nbk = (cc + cpb - 1) // cpb
    bend = jnp.cumsum(nbk)
    bstart = bend - nbk
    nb_total = bend[-1:]
    nb_max = -(-(n_tiles * USED_CHUNKS_MAX) // cpb) + E
    bidx = jnp.arange(nb_max, dtype=I32)
    be = jnp.minimum(jnp.sum((bidx[:, None] >= bend[None, :]).astype(I32), axis=1), E - 1)
    j = jnp.arange(cpb, dtype=I32)[None, :]
    pos = (bidx - bstart[be])[:, None] * cpb + j
    valid = (pos < cc[be][:, None]) & (bidx < nb_total[0])[:, None]
    sel = (be[:, None] == e_ids[None, :])[:, None, :]
    pick = lambda per_tile: jnp.sum(jnp.where(sel, per_tile[None], 0), axis=-1)
    ex_b = pick(excl)
    n8_b = pick(n8)
    base_b = pick(jnp.arange(n_tiles, dtype=I32)[:, None] * SORT_CHUNKS + lo8 - excl)
    p3 = pos[:, :, None]
    hit = (ex_b[:, None, :] <= p3) & (p3 < (ex_b + n8_b)[:, None, :])
    src = jnp.sum(jnp.where(hit, base_b[:, None, :] + p3, 0), axis=-1)
    trash = be[:, None] * cpb + j
    slots = jnp.where(valid, src, n_tiles * SORT_CHUNKS + trash).astype(I32)
    return be, nb_total.astype(I32), slots.reshape(-1)


def kernel(x, c, w_ada, b_ada, norm1_g, w_in, q_norm_g, k_norm_g, lambda_q1, lambda_k1, lambda_q2,
           lambda_k2, subln_g, conv_w, conv_b, conv_ln_g, conv_ln_b, w_out, norm2_g, w_router, b_router,
           w_gate_up, b_gate_up, w_down, b_down):
    B, S, D = x.shape
    N = B * S
    assert S % INPROJ_ROWS == 0 and S % ATTN_SUB_ROWS == 0 and S % TOK_TILE == 0
    n_tiles = N // TOK_TILE
    trash_tiles = _round_up(-(-(N_EXPERTS * CHUNKS_PER_BLOCK) // SORT_CHUNKS), ROUTE_TILES_PER_STEP)
    log2e = math.log2(math.e)
    slopes, qfeat, kfeat = _alibi_constants(S)
    xt = x.reshape(N, D)
    for l in range(w_ada.shape[0]):
        lam_init = 0.8 - 0.6 * math.exp(-0.3 * l)
        row = lambda a: a[l][None, :]
        mod, lam = _ada(c, w_ada[l], row(b_ada), row(lambda_q1), row(lambda_k1), row(lambda_q2),
                        row(lambda_k2), lam_init)
        mod3 = mod.reshape(B, N_MOD, D)
        qg = jnp.tile(q_norm_g[l], 2 * ATTN_HEADS)[None, :] * (ATTN_HEAD_DIM ** -0.5 * log2e)
        kg = jnp.tile(k_norm_g[l], 2 * ATTN_HEADS)[None, :]
        hglu, q, k, v = _inproj(xt, mod3, row(norm1_g), w_in[l].astype(BF16), qg, kg, S)
        attn = _attention(lam[0, 0:1], slopes, qfeat, kfeat, q, k, v, row(subln_g), B, S, lam_init)
        cw = jnp.pad(conv_w[l], ((0, 1), (0, 0)))
        x1 = _mixout(hglu, attn, xt, mod3, cw, row(conv_b), row(conv_ln_g), row(conv_ln_b),
                     w_out[l].astype(BF16), S)
        xs, rt, ce = _route(x1, mod3, row(norm2_g), w_router[l].T, b_router[l][:, None], S, trash_tiles)
        be, nb_total, slots = _block_tables(ce[:n_tiles], n_tiles)
        ys = _experts(be, nb_total, slots, xs, w_gate_up[l], b_gate_up[l][:, None, :],
                      w_down[l], b_down[l][:, None, :])
        rt_tok = rt[:n_tiles].transpose(0, 2, 1).reshape(N, 2 * TOP_K)
        xt = _combine(ys, rt_tok, x1, mod3, S)
    return xt.reshape(B, S, D)
```

```python
import functools
import math

import jax
import jax.numpy as jnp
import numpy as np
from jax import lax
from jax.experimental import pallas as pl
from jax.experimental.pallas import tpu as pltpu

F32 = jnp.float32
BF16 = jnp.bfloat16
I32 = jnp.int32
U32 = jnp.uint32
HIGHEST = lax.Precision.HIGHEST

CONV_CH = 512
CONV_KERNEL = 31
CONV_HALO = 16
ATTN_HEADS = 4
ATTN_HEAD_DIM = 64
ATTN_V_DIM = 2 * ATTN_HEAD_DIM
ATTN_WIDTH = ATTN_HEADS * ATTN_V_DIM
N_EXPERTS = 32
TOP_K = 4
SWIGLU_LIMIT = 7.0
SWIGLU_ALPHA = 1.702
N_MOD = 6
EPS = 1e-5

V7X_LANES = 128
V7X_SUBLANES = 8

INPROJ_ROWS = 1024
ATTN_SUB_ROWS = 256
MIXOUT_ROWS = 1024
TOK_TILE = 256
COMBINE_TILES_PER_STEP = 4
ROUTE_TILES_PER_STEP = 4
CHUNK = V7X_SUBLANES
BLOCK_ROWS = 512
CHUNKS_PER_BLOCK = BLOCK_ROWS // CHUNK


def _round_up(a, b):
    return (a + b - 1) // b * b


USED_CHUNKS_MAX = (TOP_K * TOK_TILE + N_EXPERTS * (CHUNK - 1)) // CHUNK
SORT_ROWS = _round_up(USED_CHUNKS_MAX * CHUNK, V7X_LANES)
SORT_CHUNKS = SORT_ROWS // CHUNK


def _sigmoid(v):
    return 1.0 / (1.0 + jnp.exp(-v))


def _nt_dims():
    return (((1,), (1,)), ((), ()))


def _pack_halves(v, *, is_bf16_exact):
    c = v.shape[1] // 2
    bits = lax.bitcast_convert_type(v, U32)
    if not is_bf16_exact:
        bits = bits + jnp.uint32(0x7FFF) + ((bits >> 16) & jnp.uint32(1))
    return (bits[:, :c] >> 16) | (bits[:, c:] & jnp.uint32(0xFFFF0000))


def _unpack_halves(u):
    lo = lax.bitcast_convert_type(u << 16, F32).astype(BF16)
    hi = lax.bitcast_convert_type(u & jnp.uint32(0xFFFF0000), F32).astype(BF16)
    return jnp.concatenate([lo, hi], axis=1)


def _ada_kernel(c_ref, w_ref, b_ref, lq1_ref, lk1_ref, lq2_ref, lk2_ref, mod_ref, lam_ref, *, lam_init):
    c = c_ref[...]
    sc = c * _sigmoid(c)
    mod_ref[...] = jnp.dot(sc, w_ref[...], precision=HIGHEST, preferred_element_type=F32) + b_ref[...]
    s1 = jnp.sum(lq1_ref[...] * lk1_ref[...], axis=-1, keepdims=True)
    s2 = jnp.sum(lq2_ref[...] * lk2_ref[...], axis=-1, keepdims=True)
    lam = jnp.exp(s1) - jnp.exp(s2) + lam_init
    lam_ref[...] = jnp.broadcast_to(lam, lam_ref.shape)


def _ada(c, w_ada, b_ada, lq1, lk1, lq2, lk2, lam_init):
    B, D = c.shape
    cols = w_ada.shape[1]
    bc = D
    vec = pl.BlockSpec((1, ATTN_HEAD_DIM), lambda j: (0, 0))
    return pl.pallas_call(
        functools.partial(_ada_kernel, lam_init=lam_init),
        name="ada",
        grid=(cols // bc,),
        in_specs=[
            pl.BlockSpec((B, D), lambda j: (0, 0)),
            pl.BlockSpec((D, bc), lambda j: (0, j)),
            pl.BlockSpec((1, bc), lambda j: (0, j)),
            vec, vec, vec, vec,
        ],
        out_specs=[
            pl.BlockSpec((B, bc), lambda j: (0, j)),
            pl.BlockSpec((V7X_SUBLANES, V7X_LANES), lambda j: (0, 0)),
        ],
        out_shape=[
            jax.ShapeDtypeStruct((B, cols), F32),
            jax.ShapeDtypeStruct((V7X_SUBLANES, V7X_LANES), F32),
        ],
        compiler_params=pltpu.CompilerParams(dimension_semantics=("arbitrary",)),
    )(c, w_ada, b_ada, lq1, lk1, lq2, lk2)


def _group_rms_scale(t, lo):
    sq = t * t
    s1 = jnp.sum(jnp.where(lo, sq, 0.0), axis=-1, keepdims=True)
    s2 = jnp.sum(jnp.where(lo, 0.0, sq), axis=-1, keepdims=True)
    r1 = lax.rsqrt(s1 * (1.0 / ATTN_HEAD_DIM) + EPS)
    r2 = lax.rsqrt(s2 * (1.0 / ATTN_HEAD_DIM) + EPS)
    return jnp.where(lo, r1, r2)


def _inproj_kernel(x_ref, mod_ref, g1_ref, w_ref, qg_ref, kg_ref, hglu_ref, q_ref, k_ref, v_ref):
    x = x_ref[...]
    ms = jnp.mean(x * x, axis=-1, keepdims=True)
    shift = mod_ref[0:1, :]
    scale = mod_ref[1:2, :]
    h = ((x * lax.rsqrt(ms + EPS)) * g1_ref[...]) * (1.0 + scale) + shift
    hb = h.astype(BF16)
    c2 = 2 * CONV_CH
    ag = jnp.dot(hb, w_ref[:, 0:c2], preferred_element_type=F32)
    hglu_ref[...] = ag[:, :CONV_CH] * _sigmoid(ag[:, CONV_CH:])
    lo = lax.broadcasted_iota(I32, (1, V7X_LANES), 1) < ATTN_HEAD_DIM
    for src_col, g_ref, o_ref in ((c2, qg_ref, q_ref), (c2 + ATTN_WIDTH, kg_ref, k_ref)):
        t = jnp.dot(hb, w_ref[:, src_col:src_col + ATTN_WIDTH], preferred_element_type=F32)
        for hd in range(ATTN_HEADS):
            sl = slice(hd * V7X_LANES, (hd + 1) * V7X_LANES)
            th = t[:, sl]
            o_ref[:, sl] = (th * _group_rms_scale(th, lo) * g_ref[:, sl]).astype(BF16)
    v0 = c2 + 2 * ATTN_WIDTH
    v_ref[...] = jnp.dot(hb, w_ref[:, v0:v0 + ATTN_WIDTH], preferred_element_type=F32).astype(BF16)


def _inproj(x2d, mod3, g1, w_in_b, qg, kg, seq):
    N, D = x2d.shape
    tm = INPROJ_ROWS
    steps_per_seq = seq // tm
    cols = w_in_b.shape[1]
    row = lambda i: (i, 0)
    const = lambda i: (0, 0)
    return pl.pallas_call(
        _inproj_kernel,
        name="inproj",
        grid=(N // tm,),
        in_specs=[
            pl.BlockSpec((tm, D), row),
            pl.BlockSpec((None, N_MOD, D), lambda i: (i // steps_per_seq, 0, 0)),
            pl.BlockSpec((1, D), const),
            pl.BlockSpec((D, cols), const),
            pl.BlockSpec((1, ATTN_WIDTH), const),
            pl.BlockSpec((1, ATTN_WIDTH), const),
        ],
        out_specs=[
            pl.BlockSpec((tm, CONV_CH), row),
            pl.BlockSpec((tm, ATTN_WIDTH), row),
            pl.BlockSpec((tm, ATTN_WIDTH), row),
            pl.BlockSpec((tm, ATTN_WIDTH), row),
        ],
        out_shape=[
            jax.ShapeDtypeStruct((N, CONV_CH), F32),
            jax.ShapeDtypeStruct((N, ATTN_WIDTH), BF16),
            jax.ShapeDtypeStruct((N, ATTN_WIDTH), BF16),
            jax.ShapeDtypeStruct((N, ATTN_WIDTH), BF16),
        ],
        compiler_params=pltpu.CompilerParams(
            dimension_semantics=("parallel",), vmem_limit_bytes=48 * 1024 * 1024),
    )(x2d, mod3, g1, w_in_b, qg, kg)


ATTN_FEAT = 9


def _split3_bf16(x):
    def trunc(v):
        return (np.asarray(v, np.float32).view(np.uint32) & np.uint32(0xFFFF0000)).view(np.float32)
    x = np.asarray(x, np.float32)
    x0 = trunc(x)
    x1 = trunc(x - x0)
    x2 = trunc(x - x0 - x1)
    return [x0, x1, x2]


def _alibi_constants(seq):
    slopes = (np.exp2(-8.0 * np.arange(1, ATTN_HEADS + 1) / ATTN_HEADS) * math.log2(math.e)).astype(np.float32)
    pos = np.arange(seq)
    a = (pos // 16).astype(np.float32)
    b = (pos % 16).astype(np.float32)
    one = np.ones((seq,), np.float32)
    qfeat = np.zeros((seq, V7X_LANES), np.float32)
    qfeat[:, :ATTN_FEAT] = np.stack([a, a, a, b, b, b, one, one, one], axis=1)
    kfeat = np.zeros((ATTN_HEADS, seq, V7X_LANES), np.float32)
    for h in range(ATTN_HEADS):
        sl = slopes[h]
        cols = ([-c * one for c in _split3_bf16(np.float32(16.0) * sl)] + [-c * one for c in _split3_bf16(sl)]
                + _split3_bf16(sl * pos.astype(np.float32)))
        kfeat[h, :, :ATTN_FEAT] = np.stack(cols, axis=1)
    return jnp.asarray(slopes), jnp.asarray(qfeat, BF16), jnp.asarray(kfeat, BF16)


def _attn_kernel(lam_ref, slope_ref, q_ref, qf_ref, k_ref, kf_ref, v_ref, sg_ref, o_ref, *, ts, lam_init):
    hd = pl.program_id(1)
    lam = lam_ref[0]
    slope = slope_ref[hd]
    S = k_ref.shape[0]
    lo = lax.broadcasted_iota(I32, (1, V7X_LANES), 1) < ATTN_HEAD_DIM
    rel = lax.broadcasted_iota(I32, (ts, ts), 0) - lax.broadcasted_iota(I32, (ts, ts), 1)
    dbias = jnp.abs(rel).astype(F32) * slope
    k = k_ref[...]
    kf = kf_ref[...]
    v1 = jnp.concatenate([v_ref[...], jnp.ones((S, ATTN_V_DIM), BF16)], axis=1)
    k_left = jnp.concatenate([k, kf], axis=1)
    k_right = jnp.concatenate([k, -kf], axis=1)
    dn = _nt_dims()

    for t in range(S // ts):
        rows = slice(t * ts, (t + 1) * ts)
        q = q_ref[rows, :]
        qf = qf_ref[rows, :]
        zero = jnp.zeros_like(q)
        outs = []
        for qc in (jnp.where(lo, q, zero), jnp.where(lo, zero, q)):
            qa = jnp.concatenate([qc, qf], axis=1)
            parts = [(lax.dot_general(qc, k_ref[rows, :], dn, preferred_element_type=F32) - dbias, v1[rows])]
            if t > 0:
                parts.append((lax.dot_general(qa, k_left[:t * ts], dn, preferred_element_type=F32),
                              v1[:t * ts]))
            if (t + 1) * ts < S:
                parts.append((lax.dot_general(qa, k_right[(t + 1) * ts:], dn, preferred_element_type=F32),
                              v1[(t + 1) * ts:]))
            m = parts[0][0].max(axis=-1, keepdims=True)
            for sp, _ in parts[1:]:
                m = jnp.maximum(m, sp.max(axis=-1, keepdims=True))
            acc = jnp.zeros((ts, 2 * ATTN_V_DIM), F32)
            for sp, vp in parts:
                acc = acc + jnp.dot(jnp.exp2(sp - m).astype(BF16), vp, preferred_element_type=F32)
            outs.append((acc[:, :ATTN_V_DIM], acc[:, ATTN_V_DIM:ATTN_V_DIM + 1]))
        (a1, l1), (a2, l2) = outs
        o = a1 * (1.0 / l1) - a2 * (lam / l2)
        ms = jnp.mean(o * o, axis=-1, keepdims=True)
        o = (o * lax.rsqrt(ms + EPS)) * sg_ref[...] * (1.0 - lam_init)
        o_ref[rows, :] = o.astype(BF16)


def _attention(lam1, slopes, qfeat, kfeat, q, k, v, sg, batch, seq, lam_init):
    smem = pl.BlockSpec(memory_space=pltpu.SMEM)
    seq_blk = lambda b, h: (b, h)
    return pl.pallas_call(
        functools.partial(_attn_kernel, ts=ATTN_SUB_ROWS, lam_init=lam_init),
        name="attn",
        grid=(batch, ATTN_HEADS),
        in_specs=[
            smem, smem,
            pl.BlockSpec((seq, ATTN_V_DIM), seq_blk),
            pl.BlockSpec((seq, V7X_LANES), lambda b, h: (0, 0)),
            pl.BlockSpec((seq, ATTN_V_DIM), seq_blk),
            pl.BlockSpec((None, seq, V7X_LANES), lambda b, h: (h, 0, 0)),
            pl.BlockSpec((seq, ATTN_V_DIM), seq_blk),
            pl.BlockSpec((1, ATTN_V_DIM), lambda b, h: (0, 0)),
        ],
        out_specs=pl.BlockSpec((seq, ATTN_V_DIM), seq_blk),
        out_shape=jax.ShapeDtypeStruct(q.shape, BF16),
        compiler_params=pltpu.CompilerParams(
            dimension_semantics=("parallel", "parallel"), vmem_limit_bytes=48 * 1024 * 1024),
    )(lam1, slopes, q, qfeat, k, kfeat, v, sg)


def _mixout_kernel(prev_ref, main_ref, next_ref, attn_ref, x_ref, mod_ref, cw_ref, cb_ref, lg_ref, lb_ref,
                   wo_ref, x1_ref, win_ref, sh_ref, *, tiles_per_seq):
    T = main_ref.shape[0]
    j = lax.rem(pl.program_id(0), tiles_per_seq)
    H = CONV_HALO
    win_ref[0:H, :] = jnp.where(j == 0, 0.0, prev_ref[...])
    win_ref[H:H + T, :] = main_ref[...]
    win_ref[H + T:H + T + H, :] = jnp.where(j == tiles_per_seq - 1, 0.0, next_ref[...])
    rows = sh_ref.shape[1]
    for r in range(1, V7X_SUBLANES):
        sh_ref[r] = win_ref[r:r + rows, :]
    acc = jnp.broadcast_to(cb_ref[...], (T, CONV_CH))
    off = H - CONV_KERNEL // 2
    for tap in range(CONV_KERNEL):
        a, r = divmod(off + tap, V7X_SUBLANES)
        src = win_ref if r == 0 else sh_ref.at[r]
        acc = acc + cw_ref[tap:tap + 1, :] * src[a * V7X_SUBLANES:a * V7X_SUBLANES + T, :]
    mu = jnp.mean(acc, axis=-1, keepdims=True)
    xc = acc - mu
    var = jnp.mean(xc * xc, axis=-1, keepdims=True)
    y = (xc * lax.rsqrt(var + EPS)) * lg_ref[...] + lb_ref[...]
    y = y * _sigmoid(y)
    mix = jnp.dot(y.astype(BF16), wo_ref[0:CONV_CH, :], preferred_element_type=F32)
    mix = mix + jnp.dot(attn_ref[...], wo_ref[CONV_CH:CONV_CH + ATTN_WIDTH, :], preferred_element_type=F32)
    x1_ref[...] = x_ref[...] + mod_ref[2:3, :] * mix


def _mixout(hglu, attn, x2d, mod3, cw, cb, lg, lb, wo_b, seq):
    N, D = x2d.shape
    T = MIXOUT_ROWS
    tiles_per_seq = seq // T
    hpt = T // CONV_HALO
    n_halo = N // CONV_HALO
    row = lambda i: (i, 0)
    const = lambda i: (0, 0)
    return pl.pallas_call(
        functools.partial(_mixout_kernel, tiles_per_seq=tiles_per_seq),
        name="mixout",
        grid=(N // T,),
        in_specs=[
            pl.BlockSpec((CONV_HALO, CONV_CH), lambda i: (jnp.maximum(i * hpt - 1, 0), 0)),
            pl.BlockSpec((T, CONV_CH), row),
            pl.BlockSpec((CONV_HALO, CONV_CH), lambda i: (jnp.minimum((i + 1) * hpt, n_halo - 1), 0)),
            pl.BlockSpec((T, ATTN_WIDTH), row),
            pl.BlockSpec((T, D), row),
            pl.BlockSpec((None, N_MOD, D), lambda i: (i // tiles_per_seq, 0, 0)),
            pl.BlockSpec(cw.shape, const),
            pl.BlockSpec((1, CONV_CH), const),
            pl.BlockSpec((1, CONV_CH), const),
            pl.BlockSpec((1, CONV_CH), const),
            pl.BlockSpec(wo_b.shape, const),
        ],
        out_specs=pl.BlockSpec((T, D), row),
        out_shape=jax.ShapeDtypeStruct((N, D), F32),
        scratch_shapes=[
            pltpu.VMEM((T + 2 * CONV_HALO, CONV_CH), F32),
            pltpu.VMEM((V7X_SUBLANES, T + 2 * CONV_HALO - V7X_SUBLANES, CONV_CH), F32),
        ],
        compiler_params=pltpu.CompilerParams(
            dimension_semantics=("parallel",), vmem_limit_bytes=48 * 1024 * 1024),
    )(hglu, hglu, hglu, attn, x2d, mod3, cw, cb, lg, lb, wo_b)


def _route_kernel(x1_ref, mod_ref, g2_ref, wr_ref, br_ref, xs_ref, rt_ref, ce_ref, *, n_steps):
    i = pl.program_id(0)

    @pl.when(i < n_steps)
    def _():
        _route_tiles(x1_ref, mod_ref, g2_ref, wr_ref, br_ref, xs_ref, rt_ref, ce_ref)

    @pl.when(i >= n_steps)
    def _():
        xs_ref[...] = jnp.zeros(xs_ref.shape, U32)
        rt_ref[...] = jnp.zeros(rt_ref.shape, F32)
        ce_ref[...] = jnp.full(ce_ref.shape, N_EXPERTS, I32)


def _route_tiles(x1_ref, mod_ref, g2_ref, wr_ref, br_ref, xs_ref, rt_ref, ce_ref):
    G, T, E, L = ROUTE_TILES_PER_STEP, TOK_TILE, N_EXPERTS, SORT_ROWS
    W = G * T
    x1 = x1_ref[...]
    ms = jnp.mean(x1 * x1, axis=-1, keepdims=True)
    h2 = ((x1 * lax.rsqrt(ms + EPS)) * g2_ref[...]) * (1.0 + mod_ref[4:5, :]) + mod_ref[3:4, :]
    h2b = h2.astype(BF16)
    logits = lax.dot_general(wr_ref[...], h2, _nt_dims(), precision=HIGHEST,
                             preferred_element_type=F32) + br_ref[...]
    e_iota = lax.broadcasted_iota(I32, (E, W), 0)
    sels, tops = [], []
    l = logits
    for _ in range(TOP_K):
        m = jnp.max(l, axis=0, keepdims=True)
        idx = jnp.min(jnp.where(l == m, e_iota, E), axis=0, keepdims=True)
        sel = e_iota == idx
        l = jnp.where(sel, -jnp.inf, l)
        sels.append(sel)
        tops.append(m)
    ws = [jnp.exp(m - tops[0]) for m in tops]
    den = ws[0] + ws[1] + ws[2] + ws[3]
    gates = [w / den for w in ws]
    multi = jnp.zeros((E, W), F32)
    for sel in sels:
        multi = multi + jnp.where(sel, 1.0, 0.0)
    multi_b = multi.astype(BF16)
    r_iota = lax.broadcasted_iota(I32, (T, T + V7X_LANES), 0)
    c_iota = lax.broadcasted_iota(I32, (T, T + V7X_LANES), 1)
    tri = jnp.where((r_iota < c_iota) | (c_iota >= T), 1.0, 0.0).astype(BF16)
    rks = [jnp.dot(multi_b[:, h * T:(h + 1) * T], tri, preferred_element_type=F32) for h in range(G)]
    cnt = jnp.concatenate([rk[:, T:T + V7X_LANES] for rk in rks], axis=1)
    n8 = jnp.floor((cnt + (CHUNK - 1)) * (1.0 / CHUNK))
    e_iota_l = lax.broadcasted_iota(I32, n8.shape, 0)
    lo8 = jnp.zeros(n8.shape, F32)
    for e in range(E - 1):
        lo8 = lo8 + jnp.where(e_iota_l > e, n8[e:e + 1, :], 0.0)
    s_iota = lax.broadcasted_iota(I32, (E, ce_ref.shape[2]), 1).astype(F32)
    j_iota = lax.broadcasted_iota(I32, (L, T), 0).astype(F32)
    for h in range(G):
        cols = slice(h * T, (h + 1) * T)
        lo_h = lo8[:, h * V7X_LANES:h * V7X_LANES + 1]
        n_h = n8[:, h * V7X_LANES:h * V7X_LANES + 1]
        base = lo_h * float(CHUNK) + rks[h][:, :T]
        dests = [jnp.sum(jnp.where(sel[:, cols], base, 0.0), axis=0, keepdims=True) for sel in sels]
        for k in range(TOP_K):
            rt_ref[h, k:k + 1, :] = dests[k]
            rt_ref[h, TOP_K + k:TOP_K + k + 1, :] = gates[k][:, cols]
        ce_ref[h] = jnp.sum(jnp.where(lo_h + n_h <= s_iota, 1.0, 0.0), axis=0, keepdims=True).astype(I32)
        pm = jnp.zeros((L, T), F32)
        for d in dests:
            pm = pm + jnp.where(j_iota == d, 1.0, 0.0)
        xs_ref[h * L:(h + 1) * L, :] = _pack_halves(
            jnp.dot(pm.astype(BF16), h2b[h * T:(h + 1) * T, :], preferred_element_type=F32), is_bf16_exact=True)


def _route(x1, mod3, g2, wr_t, br, seq, trash_tiles):
    N, D = x1.shape
    T = TOK_TILE
    tiles_per_seq = seq // T
    n_tiles = N // T
    n_ext = n_tiles + trash_tiles
    L = SORT_ROWS
    G = ROUTE_TILES_PER_STEP
    assert tiles_per_seq % G == 0 and n_ext % G == 0
    n_steps = n_tiles // G
    ce_w = _round_up(SORT_CHUNKS, V7X_LANES)
    const = lambda i: (0, 0)
    real = lambda i: jnp.minimum(i, n_steps - 1)
    return pl.pallas_call(
        functools.partial(_route_kernel, n_steps=n_steps),
        name="route",
        grid=(n_ext // G,),
        in_specs=[
            pl.BlockSpec((G * T, D), lambda i: (real(i), 0)),
            pl.BlockSpec((None, N_MOD, D), lambda i: (real(i) // (tiles_per_seq // G), 0, 0)),
            pl.BlockSpec((1, D), const),
            pl.BlockSpec(wr_t.shape, const),
            pl.BlockSpec(br.shape, const),
        ],
        out_specs=[
            pl.BlockSpec((G * L, D // 2), lambda i: (i, 0)),
            pl.BlockSpec((G, 2 * TOP_K, T), lambda i: (i, 0, 0)),
            pl.BlockSpec((G, 1, ce_w), lambda i: (i, 0, 0)),
        ],
        out_shape=[
            jax.ShapeDtypeStruct((n_ext * L, D // 2), U32),
            jax.ShapeDtypeStruct((n_ext, 2 * TOP_K, T), F32),
            jax.ShapeDtypeStruct((n_ext, 1, ce_w), I32),
        ],
        compiler_params=pltpu.CompilerParams(
            dimension_semantics=("parallel",), vmem_limit_bytes=48 * 1024 * 1024),
    )(x1, mod3, g2, wr_t, br)


def _expert_kernel(be_ref, nb_ref, slot_ref, xs_hbm, wgu_ref, bgu_ref, wd_ref, bd_ref, ys_hbm,
                   xbuf, obuf, wgu_b, wd_b, sem_in, sem_out):
    b = pl.program_id(0)
    nb = nb_ref[0]
    dff = wd_ref.shape[0]

    def gather_copy(blk, slot, j):
        chunk = slot_ref[blk * CHUNKS_PER_BLOCK + j]
        return pltpu.make_async_copy(xs_hbm.at[chunk], xbuf.at[slot, pl.ds(j * CHUNK, CHUNK)], sem_in.at[slot])

    def scatter_copy(blk, slot, j):
        chunk = slot_ref[blk * CHUNKS_PER_BLOCK + j]
        return pltpu.make_async_copy(obuf.at[slot, pl.ds(j * CHUNK, CHUNK)], ys_hbm.at[chunk], sem_out.at[slot])

    def start_gather(blk, slot):
        for j in range(CHUNKS_PER_BLOCK):
            gather_copy(blk, slot, j).start()

    def wait_gather(blk, slot):
        for j in range(CHUNKS_PER_BLOCK):
            gather_copy(blk, slot, j).wait()

    def start_scatter(blk, slot):
        for j in range(CHUNKS_PER_BLOCK):
            scatter_copy(blk, slot, j).start()

    def wait_scatter(blk, slot):
        for j in range(CHUNKS_PER_BLOCK):
            scatter_copy(blk, slot, j).wait()

    @pl.when(b < nb)
    def _():
        slot = lax.rem(b, 2)

        @pl.when(b == 0)
        def _():
            start_gather(0, 0)

        @pl.when(b + 1 < nb)
        def _():
            start_gather(b + 1, 1 - slot)

        @pl.when((b == 0) | (be_ref[b] != be_ref[jnp.maximum(b - 1, 0)]))
        def _():
            wgu_b[...] = wgu_ref[...].astype(BF16)
            wd_b[...] = wd_ref[...].astype(BF16)

        wait_gather(b, slot)

        @pl.when(b >= 2)
        def _():
            wait_scatter(b - 2, slot)

        xb = _unpack_halves(xbuf[slot])
        gu = jnp.dot(xb, wgu_b[...], preferred_element_type=F32) + bgu_ref[...]
        g = jnp.minimum(gu[:, :dff], SWIGLU_LIMIT)
        u = jnp.clip(gu[:, dff:], -SWIGLU_LIMIT, SWIGLU_LIMIT)
        y = (u + 1.0) * (g * _sigmoid(SWIGLU_ALPHA * g))
        out = jnp.dot(y.astype(BF16), wd_b[...], preferred_element_type=F32) + bd_ref[...]
        obuf[slot] = _pack_halves(out, is_bf16_exact=False)
        start_scatter(b, slot)

        @pl.when(b == nb - 1)
        def _():
            wait_scatter(b, slot)

            @pl.when(b >= 1)
            def _():
                wait_scatter(b - 1, 1 - slot)


def _experts(blk_expert, nb_total, slots, xs, wgu, bgu, wd, bd):
    nb_max = blk_expert.shape[0]
    half = xs.shape[1]
    D = 2 * half
    f2 = wgu.shape[2]
    dff = wd.shape[1]
    chunks = xs.reshape(xs.shape[0] // CHUNK, CHUNK, half)
    wmap = lambda b, be, nb, sl: (be[b], 0, 0)
    grid_spec = pltpu.PrefetchScalarGridSpec(
        num_scalar_prefetch=3,
        grid=(nb_max,),
        in_specs=[
            pl.BlockSpec(memory_space=pl.ANY),
            pl.BlockSpec((None, D, f2), wmap),
            pl.BlockSpec((None, 1, f2), wmap),
            pl.BlockSpec((None, dff, D), wmap),
            pl.BlockSpec((None, 1, D), wmap),
        ],
        out_specs=pl.BlockSpec(memory_space=pl.ANY),
        scratch_shapes=[
            pltpu.VMEM((2, BLOCK_ROWS, half), U32),
            pltpu.VMEM((2, BLOCK_ROWS, half), U32),
            pltpu.VMEM((D, f2), BF16),
            pltpu.VMEM((dff, D), BF16),
            pltpu.SemaphoreType.DMA((2,)),
            pltpu.SemaphoreType.DMA((2,)),
        ],
    )
    ys = pl.pallas_call(
        _expert_kernel,
        name="experts",
        grid_spec=grid_spec,
        out_shape=jax.ShapeDtypeStruct(chunks.shape, U32),
        input_output_aliases={3: 0},
        compiler_params=pltpu.CompilerParams(
            dimension_semantics=("arbitrary",), vmem_limit_bytes=56 * 1024 * 1024),
    )(blk_expert, nb_total, slots, chunks, wgu, bgu, wd, bd)
    return ys.reshape(xs.shape)


def _combine_kernel(ys_ref, rt_ref, x1_ref, mod_ref, x2_ref):
    T, L = TOK_TILE, SORT_ROWS
    j_iota = lax.broadcasted_iota(I32, (T, L), 1).astype(F32)
    for h in range(COMBINE_TILES_PER_STEP):
        rows = slice(h * T, (h + 1) * T)
        wm = jnp.zeros((T, L), F32)
        for k in range(TOP_K):
            wm = wm + jnp.where(j_iota == rt_ref[rows, k:k + 1], rt_ref[rows, TOP_K + k:TOP_K + k + 1], 0.0)
        y = jnp.dot(wm.astype(BF16), _unpack_halves(ys_ref[h * L:(h + 1) * L, :]), preferred_element_type=F32)
        x2_ref[rows, :] = x1_ref[rows, :] + mod_ref[5:6, :] * y


def _combine(ys, rt, x1, mod3, seq):
    N, D = x1.shape
    G = COMBINE_TILES_PER_STEP
    T = G * TOK_TILE
    tiles_per_seq = seq // T
    L = G * SORT_ROWS
    assert seq % T == 0
    return pl.pallas_call(
        _combine_kernel,
        name="combine",
        grid=(N // T,),
        in_specs=[
            pl.BlockSpec((L, D // 2), lambda i: (i, 0)),
            pl.BlockSpec((T, 2 * TOP_K), lambda i: (i, 0)),
            pl.BlockSpec((T, D), lambda i: (i, 0)),
            pl.BlockSpec((None, N_MOD, D), lambda i: (i // tiles_per_seq, 0, 0)),
        ],
        out_specs=pl.BlockSpec((T, D), lambda i: (i, 0)),
        out_shape=jax.ShapeDtypeStruct((N, D), F32),
        compiler_params=pltpu.CompilerParams(
            dimension_semantics=("parallel",), vmem_limit_bytes=48 * 1024 * 1024),
    )(ys, rt, x1, mod3)


def _block_tables(chunk_expert, n_tiles):
    E, cpb = N_EXPERTS, CHUNKS_PER_BLOCK
    e_ids = jnp.arange(E, dtype=I32)
    n8 = jnp.sum((chunk_expert[:, 0, :SORT_CHUNKS, None] == e_ids).astype(I32), axis=1)
    lo8 = jnp.cumsum(n8, axis=1) - n8
    excl = jnp.cumsum(n8, axis=0) - n8
    cc = jnp.sum(n8, axis=0)
    nbk = (cc + cpb - 1) // cpb
    bend = jnp.cumsum(nbk)
    bstart = bend - nbk
    nb_total = bend[-1:]
    nb_max = -(-(n_tiles * USED_CHUNKS_MAX) // cpb) + E
    bidx = jnp.arange(nb_max, dtype=I32)
    be = jnp.minimum(jnp.sum((bidx[:, None] >= bend[None, :]).astype(I32), axis=1), E - 1)
    j = jnp.arange(cpb, dtype=I32)[None, :]
    pos = (bidx - bstart[be])[:, None] * cpb + j
    valid = (pos < cc[be][:, None]) & (bidx < nb_total[0])[:, None]
    sel = (be[:, None] == e_ids[None, :])[:, None, :]
    pick = lambda per_tile: jnp.sum(jnp.where(sel, per_tile[None], 0), axis=-1)
    ex_b = pick(excl)
    n8_b = pick(n8)
    base_b = pick(jnp.arange(n_tiles, dtype=I32)[:, None] * SORT_CHUNKS + lo8 - excl)
    p3 = pos[:, :, None]
    hit = (ex_b[:, None, :] <= p3) & (p3 < (ex_b + n8_b)[:, None, :])
    src = jnp.sum(jnp.where(hit, base_b[:, None, :] + p3, 0), axis=-1)
    trash = be[:, None] * cpb + j
    slots = jnp.where(valid, src, n_tiles * SORT_CHUNKS + trash).astype(I32)
    return be, nb_total.astype(I32), slots.reshape(-1)


def kernel(x, c, w_ada, b_ada, norm1_g, w_in, q_norm_g, k_norm_g, lambda_q1, lambda_k1, lambda_q2,
           lambda_k2, subln_g, conv_w, conv_b, conv_ln_g, conv_ln_b, w_out, norm2_g, w_router, b_router,
           w_gate_up, b_gate_up, w_down, b_down):
    B, S, D = x.shape
    N = B * S
    assert S % INPROJ_ROWS == 0 and S % ATTN_SUB_ROWS == 0 and S % TOK_TILE == 0
    n_tiles = N // TOK_TILE
    trash_tiles = _round_up(-(-(N_EXPERTS * CHUNKS_PER_BLOCK) // SORT_CHUNKS), ROUTE_TILES_PER_STEP)
    log2e = math.log2(math.e)
    slopes, qfeat, kfeat = _alibi_constants(S)
    xt = x.reshape(N, D)
    for l in range(w_ada.shape[0]):
        lam_init = 0.8 - 0.6 * math.exp(-0.3 * l)
        row = lambda a: a[l][None, :]
        mod, lam = _ada(c, w_ada[l], row(b_ada), row(lambda_q1), row(lambda_k1), row(lambda_q2),
                        row(lambda_k2), lam_init)
        mod3 = mod.reshape(B, N_MOD, D)
        qg = jnp.tile(q_norm_g[l], 2 * ATTN_HEADS)[None, :] * (ATTN_HEAD_DIM ** -0.5 * log2e)
        kg = jnp.tile(k_norm_g[l], 2 * ATTN_HEADS)[None, :]
        hglu, q, k, v = _inproj(xt, mod3, row(norm1_g), w_in[l].astype(BF16), qg, kg, S)
        attn = _attention(lam[0, 0:1], slopes, qfeat, kfeat, q, k, v, row(subln_g), B, S, lam_init)
        cw = jnp.pad(conv_w[l], ((0, 1), (0, 0)))
        x1 = _mixout(hglu, attn, xt, mod3, cw, row(conv_b), row(conv_ln_g), row(conv_ln_b),
                     w_out[l].astype(BF16), S)
        xs, rt, ce = _route(x1, mod3, row(norm2_g), w_router[l].T, b_router[l][:, None], S, trash_tiles)
        be, nb_total, slots = _block_tables(ce[:n_tiles], n_tiles)
        ys = _experts(be, nb_total, slots, xs, w_gate_up[l], b_gate_up[l][:, None, :],
                      w_down[l], b_down[l][:, None, :])
        rt_tok = rt[:n_tiles].transpose(0, 2, 1).reshape(N, 2 * TOP_K)
        xt = _combine(ys, rt_tok, x1, mod3, S)
    return xt.reshape(B, S, D)
```

```python
import functools
import math

import jax
import jax.numpy as jnp
import numpy as np
from jax import lax
from jax.experimental import pallas as pl
from jax.experimental.pallas import tpu as pltpu

F32 = jnp.float32
BF16 = jnp.bfloat16
I32 = jnp.int32
U32 = jnp.uint32
HIGHEST = lax.Precision.HIGHEST

CONV_CH = 512
CONV_KERNEL = 31
CONV_HALO = 16
ATTN_HEADS = 4
ATTN_HEAD_DIM = 64
ATTN_V_DIM = 2 * ATTN_HEAD_DIM
ATTN_WIDTH = ATTN_HEADS * ATTN_V_DIM
N_EXPERTS = 32
TOP_K = 4
SWIGLU_LIMIT = 7.0
SWIGLU_ALPHA = 1.702
N_MOD = 6
EPS = 1e-5

V7X_LANES = 128
V7X_SUBLANES = 8
V7X_VMEM_BYTES = 64 * 1024 * 1024
VMEM_LIMIT = V7X_VMEM_BYTES * 3 // 4
EXPERT_VMEM_LIMIT = V7X_VMEM_BYTES * 7 // 8

INPROJ_ROWS = 1024
ATTN_SUB_ROWS = 256
MIXOUT_ROWS = 1024
TOK_TILE = 256
COMBINE_TILES_PER_STEP = 4
ROUTE_TILES_PER_STEP = 4
CHUNK = V7X_SUBLANES
BLOCK_ROWS = 512
CHUNKS_PER_BLOCK = BLOCK_ROWS // CHUNK


def _round_up(a, b):
    return (a + b - 1) // b * b


USED_CHUNKS_MAX = (TOP_K * TOK_TILE + N_EXPERTS * (CHUNK - 1)) // CHUNK
SORT_ROWS = _round_up(USED_CHUNKS_MAX * CHUNK, V7X_LANES)
SORT_CHUNKS = SORT_ROWS // CHUNK


def _sigmoid(v):
    return 1.0 / (1.0 + jnp.exp(-v))


def _nt_dims():
    return (((1,), (1,)), ((), ()))


def _pack_halves(v, *, is_bf16_exact):
    c = v.shape[1] // 2
    bits = lax.bitcast_convert_type(v, U32)
    if not is_bf16_exact:
        bits = bits + jnp.uint32(0x7FFF) + ((bits >> 16) & jnp.uint32(1))
    return (bits[:, :c] >> 16) | (bits[:, c:] & jnp.uint32(0xFFFF0000))


def _unpack_halves(u):
    lo = lax.bitcast_convert_type(u << 16, F32).astype(BF16)
    hi = lax.bitcast_convert_type(u & jnp.uint32(0xFFFF0000), F32).astype(BF16)
    return jnp.concatenate([lo, hi], axis=1)


def _ada_kernel(c_ref, w_ref, b_ref, lq1_ref, lk1_ref, lq2_ref, lk2_ref, mod_ref, lam_ref, *, lam_init):
    c = c_ref[...]
    sc = c * _sigmoid(c)
    mod_ref[...] = jnp.dot(sc, w_ref[...], precision=HIGHEST, preferred_element_type=F32) + b_ref[...]
    s1 = jnp.sum(lq1_ref[...] * lk1_ref[...], axis=-1, keepdims=True)
    s2 = jnp.sum(lq2_ref[...] * lk2_ref[...], axis=-1, keepdims=True)
    lam = jnp.exp(s1) - jnp.exp(s2) + lam_init
    lam_ref[...] = jnp.broadcast_to(lam, lam_ref.shape)


def _ada(c, w_ada, b_ada, lq1, lk1, lq2, lk2, lam_init):
    B, D = c.shape
    cols = w_ada.shape[1]
    bc = D
    vec = pl.BlockSpec((1, ATTN_HEAD_DIM), lambda j: (0, 0))
    return pl.pallas_call(
        functools.partial(_ada_kernel, lam_init=lam_init),
        name="ada",
        grid=(cols // bc,),
        in_specs=[
            pl.BlockSpec((B, D), lambda j: (0, 0)),
            pl.BlockSpec((D, bc), lambda j: (0, j)),
            pl.BlockSpec((1, bc), lambda j: (0, j)),
            vec, vec, vec, vec,
        ],
        out_specs=[
            pl.BlockSpec((B, bc), lambda j: (0, j)),
            pl.BlockSpec((V7X_SUBLANES, V7X_LANES), lambda j: (0, 0)),
        ],
        out_shape=[
            jax.ShapeDtypeStruct((B, cols), F32),
            jax.ShapeDtypeStruct((V7X_SUBLANES, V7X_LANES), F32),
        ],
        compiler_params=pltpu.CompilerParams(dimension_semantics=("arbitrary",)),
    )(c, w_ada, b_ada, lq1, lk1, lq2, lk2)


def _group_rms_scale(t, lo):
    sq = t * t
    s1 = jnp.sum(jnp.where(lo, sq, 0.0), axis=-1, keepdims=True)
    s2 = jnp.sum(jnp.where(lo, 0.0, sq), axis=-1, keepdims=True)
    r1 = lax.rsqrt(s1 * (1.0 / ATTN_HEAD_DIM) + EPS)
    r2 = lax.rsqrt(s2 * (1.0 / ATTN_HEAD_DIM) + EPS)
    return jnp.where(lo, r1, r2)


def _inproj_kernel(x_ref, mod_ref, g1_ref, w_ref, qg_ref, kg_ref, hglu_ref, q_ref, k_ref, v_ref):
    x = x_ref[...]
    ms = jnp.mean(x * x, axis=-1, keepdims=True)
    shift = mod_ref[0:1, :]
    scale = mod_ref[1:2, :]
    h = ((x * lax.rsqrt(ms + EPS)) * g1_ref[...]) * (1.0 + scale) + shift
    hb = h.astype(BF16)
    c2 = 2 * CONV_CH
    ag = jnp.dot(hb, w_ref[:, 0:c2], preferred_element_type=F32)
    hglu_ref[...] = ag[:, :CONV_CH] * _sigmoid(ag[:, CONV_CH:])
    lo = lax.broadcasted_iota(I32, (1, V7X_LANES), 1) < ATTN_HEAD_DIM
    for src_col, g_ref, o_ref in ((c2, qg_ref, q_ref), (c2 + ATTN_WIDTH, kg_ref, k_ref)):
        t = jnp.dot(hb, w_ref[:, src_col:src_col + ATTN_WIDTH], preferred_element_type=F32)
        for hd in range(ATTN_HEADS):
            sl = slice(hd * V7X_LANES, (hd + 1) * V7X_LANES)
            th = t[:, sl]
            o_ref[:, sl] = (th * _group_rms_scale(th, lo) * g_ref[:, sl]).astype(BF16)
    v0 = c2 + 2 * ATTN_WIDTH
    v_ref[...] = jnp.dot(hb, w_ref[:, v0:v0 + ATTN_WIDTH], preferred_element_type=F32).astype(BF16)


def _inproj(x2d, mod3, g1, w_in_b, qg, kg, seq):
    N, D = x2d.shape
    tm = INPROJ_ROWS
    steps_per_seq = seq // tm
    cols = w_in_b.shape[1]
    row = lambda i: (i, 0)
    const = lambda i: (0, 0)
    return pl.pallas_call(
        _inproj_kernel,
        name="inproj",
        grid=(N // tm,),
        in_specs=[
            pl.BlockSpec((tm, D), row),
            pl.BlockSpec((None, N_MOD, D), lambda i: (i // steps_per_seq, 0, 0)),
            pl.BlockSpec((1, D), const),
            pl.BlockSpec((D, cols), const),
            pl.BlockSpec((1, ATTN_WIDTH), const),
            pl.BlockSpec((1, ATTN_WIDTH), const),
        ],
        out_specs=[
            pl.BlockSpec((tm, CONV_CH), row),
            pl.BlockSpec((tm, ATTN_WIDTH), row),
            pl.BlockSpec((tm, ATTN_WIDTH), row),
            pl.BlockSpec((tm, ATTN_WIDTH), row),
        ],
        out_shape=[
            jax.ShapeDtypeStruct((N, CONV_CH), F32),
            jax.ShapeDtypeStruct((N, ATTN_WIDTH), BF16),
            jax.ShapeDtypeStruct((N, ATTN_WIDTH), BF16),
            jax.ShapeDtypeStruct((N, ATTN_WIDTH), BF16),
        ],
        compiler_params=pltpu.CompilerParams(
            dimension_semantics=("parallel",), vmem_limit_bytes=VMEM_LIMIT),
    )(x2d, mod3, g1, w_in_b, qg, kg)


ATTN_FEAT = 9


def _split3_bf16(x):
    def trunc(v):
        return (np.asarray(v, np.float32).view(np.uint32) & np.uint32(0xFFFF0000)).view(np.float32)
    x = np.asarray(x, np.float32)
    x0 = trunc(x)
    x1 = trunc(x - x0)
    x2 = trunc(x - x0 - x1)
    return [x0, x1, x2]


def _alibi_constants(seq):
    slopes = (np.exp2(-8.0 * np.arange(1, ATTN_HEADS + 1) / ATTN_HEADS) * math.log2(math.e)).astype(np.float32)
    pos = np.arange(seq)
    a = (pos // 16).astype(np.float32)
    b = (pos % 16).astype(np.float32)
    one = np.ones((seq,), np.float32)
    qfeat = np.zeros((seq, V7X_LANES), np.float32)
    qfeat[:, :ATTN_FEAT] = np.stack([a, a, a, b, b, b, one, one, one], axis=1)
    kfeat = np.zeros((ATTN_HEADS, seq, V7X_LANES), np.float32)
    for h in range(ATTN_HEADS):
        sl = slopes[h]
        cols = ([-c * one for c in _split3_bf16(np.float32(16.0) * sl)] + [-c * one for c in _split3_bf16(sl)]
                + _split3_bf16(sl * pos.astype(np.float32)))
        kfeat[h, :, :ATTN_FEAT] = np.stack(cols, axis=1)
    return jnp.asarray(slopes), jnp.asarray(qfeat, BF16), jnp.asarray(kfeat, BF16)


def _attn_kernel(lam_ref, slope_ref, q_ref, qf_ref, k_ref, kf_ref, v_ref, sg_ref, o_ref, *, ts, lam_init):
    hd = pl.program_id(1)
    lam = lam_ref[0]
    slope = slope_ref[hd]
    S = k_ref.shape[0]
    lo = lax.broadcasted_iota(I32, (1, V7X_LANES), 1) < ATTN_HEAD_DIM
    rel = lax.broadcasted_iota(I32, (ts, ts), 0) - lax.broadcasted_iota(I32, (ts, ts), 1)
    dbias = jnp.abs(rel).astype(F32) * slope
    k = k_ref[...]
    kf = kf_ref[...]
    v1 = jnp.concatenate([v_ref[...], jnp.ones((S, ATTN_V_DIM), BF16)], axis=1)
    k_left = jnp.concatenate([k, kf], axis=1)
    k_right = jnp.concatenate([k, -kf], axis=1)
    dn = _nt_dims()

    for t in range(S // ts):
        rows = slice(t * ts, (t + 1) * ts)
        q = q_ref[rows, :]
        qf = qf_ref[rows, :]
        zero = jnp.zeros_like(q)
        outs = []
        for qc in (jnp.where(lo, q, zero), jnp.where(lo, zero, q)):
            qa = jnp.concatenate([qc, qf], axis=1)
            parts = [(lax.dot_general(qc, k_ref[rows, :], dn, preferred_element_type=F32) - dbias, v1[rows])]
            if t > 0:
                parts.append((lax.dot_general(qa, k_left[:t * ts], dn, preferred_element_type=F32),
                              v1[:t * ts]))
            if (t + 1) * ts < S:
                parts.append((lax.dot_general(qa, k_right[(t + 1) * ts:], dn, preferred_element_type=F32),
                              v1[(t + 1) * ts:]))
            m = parts[0][0].max(axis=-1, keepdims=True)
            for sp, _ in parts[1:]:
                m = jnp.maximum(m, sp.max(axis=-1, keepdims=True))
            acc = jnp.zeros((ts, 2 * ATTN_V_DIM), F32)
            for sp, vp in parts:
                acc = acc + jnp.dot(jnp.exp2(sp - m).astype(BF16), vp, preferred_element_type=F32)
            outs.append((acc[:, :ATTN_V_DIM], acc[:, ATTN_V_DIM:ATTN_V_DIM + 1]))
        (a1, l1), (a2, l2) = outs
        o = a1 * (1.0 / l1) - a2 * (lam / l2)
        ms = jnp.mean(o * o, axis=-1, keepdims=True)
        o = (o * lax.rsqrt(ms + EPS)) * sg_ref[...] * (1.0 - lam_init)
        o_ref[rows, :] = o.astype(BF16)


def _attention(lam1, slopes, qfeat, kfeat, q, k, v, sg, batch, seq, lam_init):
    smem = pl.BlockSpec(memory_space=pltpu.SMEM)
    seq_blk = lambda b, h: (b, h)
    return pl.pallas_call(
        functools.partial(_attn_kernel, ts=ATTN_SUB_ROWS, lam_init=lam_init),
        name="attn",
        grid=(batch, ATTN_HEADS),
        in_specs=[
            smem, smem,
            pl.BlockSpec((seq, ATTN_V_DIM), seq_blk),
            pl.BlockSpec((seq, V7X_LANES), lambda b, h: (0, 0)),
            pl.BlockSpec((seq, ATTN_V_DIM), seq_blk),
            pl.BlockSpec((None, seq, V7X_LANES), lambda b, h: (h, 0, 0)),
            pl.BlockSpec((seq, ATTN_V_DIM), seq_blk),
            pl.BlockSpec((1, ATTN_V_DIM), lambda b, h: (0, 0)),
        ],
        out_specs=pl.BlockSpec((seq, ATTN_V_DIM), seq_blk),
        out_shape=jax.ShapeDtypeStruct(q.shape, BF16),
        compiler_params=pltpu.CompilerParams(
            dimension_semantics=("parallel", "parallel"), vmem_limit_bytes=VMEM_LIMIT),
    )(lam1, slopes, q, qfeat, k, kfeat, v, sg)


def _mixout_kernel(prev_ref, main_ref, next_ref, attn_ref, x_ref, mod_ref, cw_ref, cb_ref, lg_ref, lb_ref,
                   wo_ref, x1_ref, win_ref, sh_ref, *, tiles_per_seq):
    T = main_ref.shape[0]
    j = lax.rem(pl.program_id(0), tiles_per_seq)
    H = CONV_HALO
    win_ref[0:H, :] = jnp.where(j == 0, 0.0, prev_ref[...])
    win_ref[H:H + T, :] = main_ref[...]
    win_ref[H + T:H + T + H, :] = jnp.where(j == tiles_per_seq - 1, 0.0, next_ref[...])
    rows = sh_ref.shape[1]
    for r in range(1, V7X_SUBLANES):
        sh_ref[r] = win_ref[r:r + rows, :]
    acc = jnp.broadcast_to(cb_ref[...], (T, CONV_CH))
    off = H - CONV_KERNEL // 2
    for tap in range(CONV_KERNEL):
        a, r = divmod(off + tap, V7X_SUBLANES)
        src = win_ref if r == 0 else sh_ref.at[r]
        acc = acc + cw_ref[tap:tap + 1, :] * src[a * V7X_SUBLANES:a * V7X_SUBLANES + T, :]
    mu = jnp.mean(acc, axis=-1, keepdims=True)
    xc = acc - mu
    var = jnp.mean(xc * xc, axis=-1, keepdims=True)
    y = (xc * lax.rsqrt(var + EPS)) * lg_ref[...] + lb_ref[...]
    y = y * _sigmoid(y)
    mix = jnp.dot(y.astype(BF16), wo_ref[0:CONV_CH, :], preferred_element_type=F32)
    mix = mix + jnp.dot(attn_ref[...], wo_ref[CONV_CH:CONV_CH + ATTN_WIDTH, :], preferred_element_type=F32)
    x1_ref[...] = x_ref[...] + mod_ref[2:3, :] * mix


def _mixout(hglu, attn, x2d, mod3, cw, cb, lg, lb, wo_b, seq):
    N, D = x2d.shape
    T = MIXOUT_ROWS
    tiles_per_seq = seq // T
    hpt = T // CONV_HALO
    n_halo = N // CONV_HALO
    row = lambda i: (i, 0)
    const = lambda i: (0, 0)
    return pl.pallas_call(
        functools.partial(_mixout_kernel, tiles_per_seq=tiles_per_seq),
        name="mixout",
        grid=(N // T,),
        in_specs=[
            pl.BlockSpec((CONV_HALO, CONV_CH), lambda i: (jnp.maximum(i * hpt - 1, 0), 0)),
            pl.BlockSpec((T, CONV_CH), row),
            pl.BlockSpec((CONV_HALO, CONV_CH), lambda i: (jnp.minimum((i + 1) * hpt, n_halo - 1), 0)),
            pl.BlockSpec((T, ATTN_WIDTH), row),
            pl.BlockSpec((T, D), row),
            pl.BlockSpec((None, N_MOD, D), lambda i: (i // tiles_per_seq, 0, 0)),
            pl.BlockSpec(cw.shape, const),
            pl.BlockSpec((1, CONV_CH), const),
            pl.BlockSpec((1, CONV_CH), const),
            pl.BlockSpec((1, CONV_CH), const),
            pl.BlockSpec(wo_b.shape, const),
        ],
        out_specs=pl.BlockSpec((T, D), row),
        out_shape=jax.ShapeDtypeStruct((N, D), F32),
        scratch_shapes=[
            pltpu.VMEM((T + 2 * CONV_HALO, CONV_CH), F32),
            pltpu.VMEM((V7X_SUBLANES, T + 2 * CONV_HALO - V7X_SUBLANES, CONV_CH), F32),
        ],
        compiler_params=pltpu.CompilerParams(
            dimension_semantics=("parallel",), vmem_limit_bytes=VMEM_LIMIT),
    )(hglu, hglu, hglu, attn, x2d, mod3, cw, cb, lg, lb, wo_b)


def _route_kernel(x1_ref, mod_ref, g2_ref, wr_ref, br_ref, xs_ref, rt_ref, ce_ref, *, n_steps):
    i = pl.program_id(0)

    @pl.when(i < n_steps)
    def _():
        _route_tiles(x1_ref, mod_ref, g2_ref, wr_ref, br_ref, xs_ref, rt_ref, ce_ref)

    @pl.when(i >= n_steps)
    def _():
        xs_ref[...] = jnp.zeros(xs_ref.shape, U32)
        rt_ref[...] = jnp.zeros(rt_ref.shape, F32)
        ce_ref[...] = jnp.full(ce_ref.shape, N_EXPERTS, I32)


def _route_tiles(x1_ref, mod_ref, g2_ref, wr_ref, br_ref, xs_ref, rt_ref, ce_ref):
    G, T, E, L = ROUTE_TILES_PER_STEP, TOK_TILE, N_EXPERTS, SORT_ROWS
    W = G * T
    x1 = x1_ref[...]
    ms = jnp.mean(x1 * x1, axis=-1, keepdims=True)
    h2 = ((x1 * lax.rsqrt(ms + EPS)) * g2_ref[...]) * (1.0 + mod_ref[4:5, :]) + mod_ref[3:4, :]
    h2b = h2.astype(BF16)
    logits = lax.dot_general(wr_ref[...], h2, _nt_dims(), precision=HIGHEST,
                             preferred_element_type=F32) + br_ref[...]
    e_iota = lax.broadcasted_iota(I32, (E, W), 0)
    sels, tops = [], []
    l = logits
    for _ in range(TOP_K):
        m = jnp.max(l, axis=0, keepdims=True)
        idx = jnp.min(jnp.where(l == m, e_iota, E), axis=0, keepdims=True)
        sel = e_iota == idx
        l = jnp.where(sel, -jnp.inf, l)
        sels.append(sel)
        tops.append(m)
    ws = [jnp.exp(m - tops[0]) for m in tops]
    den = ws[0] + ws[1] + ws[2] + ws[3]
    gates = [w / den for w in ws]
    multi = jnp.zeros((E, W), F32)
    for sel in sels:
        multi = multi + jnp.where(sel, 1.0, 0.0)
    multi_b = multi.astype(BF16)
    r_iota = lax.broadcasted_iota(I32, (T, T + V7X_LANES), 0)
    c_iota = lax.broadcasted_iota(I32, (T, T + V7X_LANES), 1)
    tri = jnp.where((r_iota < c_iota) | (c_iota >= T), 1.0, 0.0).astype(BF16)
    rks = [jnp.dot(multi_b[:, h * T:(h + 1) * T], tri, preferred_element_type=F32) for h in range(G)]
    cnt = jnp.concatenate([rk[:, T:T + V7X_LANES] for rk in rks], axis=1)
    n8 = jnp.floor((cnt + (CHUNK - 1)) * (1.0 / CHUNK))
    e_iota_l = lax.broadcasted_iota(I32, n8.shape, 0)
    lo8 = jnp.zeros(n8.shape, F32)
    for e in range(E - 1):
        lo8 = lo8 + jnp.where(e_iota_l > e, n8[e:e + 1, :], 0.0)
    s_iota = lax.broadcasted_iota(I32, (E, ce_ref.shape[2]), 1).astype(F32)
    j_iota = lax.broadcasted_iota(I32, (L, T), 0).astype(F32)
    for h in range(G):
        cols = slice(h * T, (h + 1) * T)
        lo_h = lo8[:, h * V7X_LANES:h * V7X_LANES + 1]
        n_h = n8[:, h * V7X_LANES:h * V7X_LANES + 1]
        base = lo_h * float(CHUNK) + rks[h][:, :T]
        dests = [jnp.sum(jnp.where(sel[:, cols], base, 0.0), axis=0, keepdims=True) for sel in sels]
        for k in range(TOP_K):
            rt_ref[h, k:k + 1, :] = dests[k]
            rt_ref[h, TOP_K + k:TOP_K + k + 1, :] = gates[k][:, cols]
        ce_ref[h] = jnp.sum(jnp.where(lo_h + n_h <= s_iota, 1.0, 0.0), axis=0, keepdims=True).astype(I32)
        pm = jnp.zeros((L, T), F32)
        for d in dests:
            pm = pm + jnp.where(j_iota == d, 1.0, 0.0)
        xs_ref[h * L:(h + 1) * L, :] = _pack_halves(
            jnp.dot(pm.astype(BF16), h2b[h * T:(h + 1) * T, :], preferred_element_type=F32), is_bf16_exact=True)


def _route(x1, mod3, g2, wr_t, br, seq, trash_tiles):
    N, D = x1.shape
    T = TOK_TILE
    tiles_per_seq = seq // T
    n_tiles = N // T
    n_ext = n_tiles + trash_tiles
    L = SORT_ROWS
    G = ROUTE_TILES_PER_STEP
    assert tiles_per_seq % G == 0 and n_ext % G == 0
    n_steps = n_tiles // G
    ce_w = _round_up(SORT_CHUNKS, V7X_LANES)
    const = lambda i: (0, 0)
    real = lambda i: jnp.minimum(i, n_steps - 1)
    return pl.pallas_call(
        functools.partial(_route_kernel, n_steps=n_steps),
        name="route",
        grid=(n_ext // G,),
        in_specs=[
            pl.BlockSpec((G * T, D), lambda i: (real(i), 0)),
            pl.BlockSpec((None, N_MOD, D), lambda i: (real(i) // (tiles_per_seq // G), 0, 0)),
            pl.BlockSpec((1, D), const),
            pl.BlockSpec(wr_t.shape, const),
            pl.BlockSpec(br.shape, const),
        ],
        out_specs=[
            pl.BlockSpec((G * L, D // 2), lambda i: (i, 0)),
            pl.BlockSpec((G, 2 * TOP_K, T), lambda i: (i, 0, 0)),
            pl.BlockSpec((G, 1, ce_w), lambda i: (i, 0, 0)),
        ],
        out_shape=[
            jax.ShapeDtypeStruct((n_ext * L, D // 2), U32),
            jax.ShapeDtypeStruct((n_ext, 2 * TOP_K, T), F32),
            jax.ShapeDtypeStruct((n_ext, 1, ce_w), I32),
        ],
        compiler_params=pltpu.CompilerParams(
            dimension_semantics=("parallel",), vmem_limit_bytes=VMEM_LIMIT),
    )(x1, mod3, g2, wr_t, br)


def _expert_kernel(be_ref, nb_ref, slot_ref, xs_hbm, wgu_ref, bgu_ref, wd_ref, bd_ref, ys_hbm,
                   xbuf, obuf, wgu_b, wd_b, sem_in, sem_out):
    b = pl.program_id(0)
    nb = nb_ref[0]
    dff = wd_ref.shape[0]

    def gather_copy(blk, slot, j):
        chunk = slot_ref[blk * CHUNKS_PER_BLOCK + j]
        return pltpu.make_async_copy(xs_hbm.at[chunk], xbuf.at[slot, pl.ds(j * CHUNK, CHUNK)], sem_in.at[slot])

    def scatter_copy(blk, slot, j):
        chunk = slot_ref[blk * CHUNKS_PER_BLOCK + j]
        return pltpu.make_async_copy(obuf.at[slot, pl.ds(j * CHUNK, CHUNK)], ys_hbm.at[chunk], sem_out.at[slot])

    def start_gather(blk, slot):
        for j in range(CHUNKS_PER_BLOCK):
            gather_copy(blk, slot, j).start()

    def wait_gather(blk, slot):
        for j in range(CHUNKS_PER_BLOCK):
            gather_copy(blk, slot, j).wait()

    def start_scatter(blk, slot):
        for j in range(CHUNKS_PER_BLOCK):
            scatter_copy(blk, slot, j).start()

    def wait_scatter(blk, slot):
        for j in range(CHUNKS_PER_BLOCK):
            scatter_copy(blk, slot, j).wait()

    @pl.when(b < nb)
    def _():
        slot = lax.rem(b, 2)

        @pl.when(b == 0)
        def _():
            start_gather(0, 0)

        @pl.when(b + 1 < nb)
        def _():
            start_gather(b + 1, 1 - slot)

        @pl.when((b == 0) | (be_ref[b] != be_ref[jnp.maximum(b - 1, 0)]))
        def _():
            wgu_b[...] = wgu_ref[...].astype(BF16)
            wd_b[...] = wd_ref[...].astype(BF16)

        wait_gather(b, slot)

        @pl.when(b >= 2)
        def _():
            wait_scatter(b - 2, slot)

        xb = _unpack_halves(xbuf[slot])
        gu = jnp.dot(xb, wgu_b[...], preferred_element_type=F32) + bgu_ref[...]
        g = jnp.minimum(gu[:, :dff], SWIGLU_LIMIT)
        u = jnp.clip(gu[:, dff:], -SWIGLU_LIMIT, SWIGLU_LIMIT)
        y = (u + 1.0) * (g * _sigmoid(SWIGLU_ALPHA * g))
        out = jnp.dot(y.astype(BF16), wd_b[...], preferred_element_type=F32) + bd_ref[...]
        obuf[slot] = _pack_halves(out, is_bf16_exact=False)
        start_scatter(b, slot)

        @pl.when(b == nb - 1)
        def _():
            wait_scatter(b, slot)

            @pl.when(b >= 1)
            def _():
                wait_scatter(b - 1, 1 - slot)


def _experts(blk_expert, nb_total, slots, xs, wgu, bgu, wd, bd):
    nb_max = blk_expert.shape[0]
    half = xs.shape[1]
    D = 2 * half
    f2 = wgu.shape[2]
    dff = wd.shape[1]
    chunks = xs.reshape(xs.shape[0] // CHUNK, CHUNK, half)
    wmap = lambda b, be, nb, sl: (be[b], 0, 0)
    grid_spec = pltpu.PrefetchScalarGridSpec(
        num_scalar_prefetch=3,
        grid=(nb_max,),
        in_specs=[
            pl.BlockSpec(memory_space=pl.ANY),
            pl.BlockSpec((None, D, f2), wmap),
            pl.BlockSpec((None, 1, f2), wmap),
            pl.BlockSpec((None, dff, D), wmap),
            pl.BlockSpec((None, 1, D), wmap),
        ],
        out_specs=pl.BlockSpec(memory_space=pl.ANY),
        scratch_shapes=[
            pltpu.VMEM((2, BLOCK_ROWS, half), U32),
            pltpu.VMEM((2, BLOCK_ROWS, half), U32),
            pltpu.VMEM((D, f2), BF16),
            pltpu.VMEM((dff, D), BF16),
            pltpu.SemaphoreType.DMA((2,)),
            pltpu.SemaphoreType.DMA((2,)),
        ],
    )
    ys = pl.pallas_call(
        _expert_kernel,
        name="experts",
        grid_spec=grid_spec,
        out_shape=jax.ShapeDtypeStruct(chunks.shape, U32),
        input_output_aliases={3: 0},
        compiler_params=pltpu.CompilerParams(
            dimension_semantics=("arbitrary",), vmem_limit_bytes=EXPERT_VMEM_LIMIT),
    )(blk_expert, nb_total, slots, chunks, wgu, bgu, wd, bd)
    return ys.reshape(xs.shape)


def _combine_kernel(ys_ref, rt_ref, x1_ref, mod_ref, x2_ref):
    T, L = TOK_TILE, SORT_ROWS
    j_iota = lax.broadcasted_iota(I32, (T, L), 1).astype(F32)
    for h in range(COMBINE_TILES_PER_STEP):
        rows = slice(h * T, (h + 1) * T)
        wm = jnp.zeros((T, L), F32)
        for k in range(TOP_K):
            wm = wm + jnp.where(j_iota == rt_ref[rows, k:k + 1], rt_ref[rows, TOP_K + k:TOP_K + k + 1], 0.0)
        y = jnp.dot(wm.astype(BF16), _unpack_halves(ys_ref[h * L:(h + 1) * L, :]), preferred_element_type=F32)
        x2_ref[rows, :] = x1_ref[rows, :] + mod_ref[5:6, :] * y


def _combine(ys, rt, x1, mod3, seq):
    N, D = x1.shape
    G = COMBINE_TILES_PER_STEP
    T = G * TOK_TILE
    tiles_per_seq = seq // T
    L = G * SORT_ROWS
    assert seq % T == 0
    return pl.pallas_call(
        _combine_kernel,
        name="combine",
        grid=(N // T,),
        in_specs=[
            pl.BlockSpec((L, D // 2), lambda i: (i, 0)),
            pl.BlockSpec((T, 2 * TOP_K), lambda i: (i, 0)),
            pl.BlockSpec((T, D), lambda i: (i, 0)),
            pl.BlockSpec((None, N_MOD, D), lambda i: (i // tiles_per_seq, 0, 0)),
        ],
        out_specs=pl.BlockSpec((T, D), lambda i: (i, 0)),
        out_shape=jax.ShapeDtypeStruct((N, D), F32),
        compiler_params=pltpu.CompilerParams(
            dimension_semantics=("parallel",), vmem_limit_bytes=VMEM_LIMIT),
    )(ys, rt, x1, mod3)


def _block_tables(chunk_expert, n_tiles):
    E, cpb = N_EXPERTS, CHUNKS_PER_BLOCK
    e_ids = jnp.arange(E, dtype=I32)
    n8 = jnp.sum((chunk_expert[:, 0, :SORT_CHUNKS, None] == e_ids).astype(I32), axis=1)
    lo8 = jnp.cumsum(n8, axis=1) - n8
    excl = jnp.cumsum(n8, axis=0) - n8
    cc = jnp.sum(n8, axis=0)
    nbk = (cc + cpb - 1) // cpb
    bend = jnp.cumsum(nbk)
    bstart = bend - nbk
    nb_total = bend[-1:]
    nb_max = -(-(n_tiles * USED_CHUNKS_MAX) // cpb) + E
    bidx = jnp.arange(nb_max, dtype=I32)
    be = jnp.minimum(jnp.sum((bidx[:, None] >= bend[None, :]).astype(I32), axis=1), E - 1)
    j = jnp.arange(cpb, dtype=I32)[None, :]
    pos = (bidx - bstart[be])[:, None] * cpb + j
    valid = (pos < cc[be][:, None]) & (bidx < nb_total[0])[:, None]
    sel = (be[:, None] == e_ids[None, :])[:, None, :]
    base = jnp.arange(n_tiles, dtype=I32)[:, None] * SORT_CHUNKS + lo8 - excl
    picked = jnp.sum(jnp.where(sel, jnp.concatenate([excl, n8, base], axis=0)[None], 0), axis=-1)
    ex_b, n8_b, base_b = jnp.split(picked, 3, axis=1)
    p3 = pos[:, :, None]
    hit = (ex_b[:, None, :] <= p3) & (p3 < (ex_b + n8_b)[:, None, :])
    src = jnp.sum(jnp.where(hit, base_b[:, None, :] + p3, 0), axis=-1)
    trash = be[:, None] * cpb + j
    slots = jnp.where(valid, src, n_tiles * SORT_CHUNKS + trash).astype(I32)
    return be, nb_total.astype(I32), slots.reshape(-1)


def kernel(x, c, w_ada, b_ada, norm1_g, w_in, q_norm_g, k_norm_g, lambda_q1, lambda_k1, lambda_q2,
           lambda_k2, subln_g, conv_w, conv_b, conv_ln_g, conv_ln_b, w_out, norm2_g, w_router, b_router,
           w_gate_up, b_gate_up, w_down, b_down):
    B, S, D = x.shape
    N = B * S
    assert S % INPROJ_ROWS == 0 and S % ATTN_SUB_ROWS == 0 and S % TOK_TILE == 0
    n_tiles = N // TOK_TILE
    trash_tiles = _round_up(-(-(N_EXPERTS * CHUNKS_PER_BLOCK) // SORT_CHUNKS), ROUTE_TILES_PER_STEP)
    log2e = math.log2(math.e)
    slopes, qfeat, kfeat = _alibi_constants(S)
    xt = x.reshape(N, D)
    for l in range(w_ada.shape[0]):
        lam_init = 0.8 - 0.6 * math.exp(-0.3 * l)
        row = lambda a: a[l][None, :]
        mod, lam = _ada(c, w_ada[l], row(b_ada), row(lambda_q1), row(lambda_k1), row(lambda_q2),
                        row(lambda_k2), lam_init)
        mod3 = mod.reshape(B, N_MOD, D)
        qg = jnp.tile(q_norm_g[l], 2 * ATTN_HEADS)[None, :] * (ATTN_HEAD_DIM ** -0.5 * log2e)
        kg = jnp.tile(k_norm_g[l], 2 * ATTN_HEADS)[None, :]
        hglu, q, k, v = _inproj(xt, mod3, row(norm1_g), w_in[l].astype(BF16), qg, kg, S)
        attn = _attention(lam[0, 0:1], slopes, qfeat, kfeat, q, k, v, row(subln_g), B, S, lam_init)
        cw = jnp.pad(conv_w[l], ((0, 1), (0, 0)))
        x1 = _mixout(hglu, attn, xt, mod3, cw, row(conv_b), row(conv_ln_g), row(conv_ln_b),
                     w_out[l].astype(BF16), S)
        xs, rt, ce = _route(x1, mod3, row(norm2_g), w_router[l].T, b_router[l][:, None], S, trash_tiles)
        be, nb_total, slots = _block_tables(ce[:n_tiles], n_tiles)
        ys = _experts(be, nb_total, slots, xs, w_gate_up[l], b_gate_up[l][:, None, :],
                      w_down[l], b_down[l][:, None, :])
        rt_tok = rt[:n_tiles].transpose(0, 2, 1).reshape(N, 2 * TOP_K)
        xt = _combine(ys, rt_tok, x1, mod3, S)
    return xt.reshape(B, S, D)
```

```python
import functools
import math

import jax
import jax.numpy as jnp
import numpy as np
from jax import lax
from jax.experimental import pallas as pl
from jax.experimental.pallas import tpu as pltpu

F32 = jnp.float32
BF16 = jnp.bfloat16
I32 = jnp.int32
U32 = jnp.uint32
HIGHEST = lax.Precision.HIGHEST

CONV_CH = 512
CONV_KERNEL = 31
CONV_HALO = 16
ATTN_HEADS = 4
ATTN_HEAD_DIM = 64
ATTN_V_DIM = 2 * ATTN_HEAD_DIM
ATTN_WIDTH = ATTN_HEADS * ATTN_V_DIM
N_EXPERTS = 32
TOP_K = 4
SWIGLU_LIMIT = 7.0
SWIGLU_ALPHA = 1.702
N_MOD = 6
EPS = 1e-5

V7X_LANES = 128
V7X_SUBLANES = 8
V7X_VMEM_BYTES = 64 * 1024 * 1024
VMEM_LIMIT = V7X_VMEM_BYTES * 3 // 4
EXPERT_VMEM_LIMIT = V7X_VMEM_BYTES * 7 // 8

INPROJ_ROWS = 1024
ATTN_SUB_ROWS = 256
MIXOUT_ROWS = 1024
TOK_TILE = 256
COMBINE_TILES_PER_STEP = 4
ROUTE_TILES_PER_STEP = 4
CHUNK = V7X_SUBLANES
BLOCK_ROWS = 512
CHUNKS_PER_BLOCK = BLOCK_ROWS // CHUNK


def _round_up(a, b):
    return (a + b - 1) // b * b


USED_CHUNKS_MAX = (TOP_K * TOK_TILE + N_EXPERTS * (CHUNK - 1)) // CHUNK
SORT_ROWS = _round_up(USED_CHUNKS_MAX * CHUNK, V7X_LANES)
SORT_CHUNKS = SORT_ROWS // CHUNK


def _sigmoid(v):
    return 1.0 / (1.0 + jnp.exp(-v))


def _nt_dims():
    return (((1,), (1,)), ((), ()))


def _pack_halves(v, *, is_bf16_exact):
    c = v.shape[1] // 2
    bits = lax.bitcast_convert_type(v, U32)
    if not is_bf16_exact:
        bits = bits + jnp.uint32(0x7FFF) + ((bits >> 16) & jnp.uint32(1))
    return (bits[:, :c] >> 16) | (bits[:, c:] & jnp.uint32(0xFFFF0000))


def _unpack_halves(u):
    lo = lax.bitcast_convert_type(u << 16, F32).astype(BF16)
    hi = lax.bitcast_convert_type(u & jnp.uint32(0xFFFF0000), F32).astype(BF16)
    return jnp.concatenate([lo, hi], axis=1)


def _ada_kernel(c_ref, w_ref, b_ref, lq1_ref, lk1_ref, lq2_ref, lk2_ref, mod_ref, lam_ref, *, lam_init):
    c = c_ref[...]
    sc = c * _sigmoid(c)
    mod_ref[...] = jnp.dot(sc, w_ref[...], precision=HIGHEST, preferred_element_type=F32) + b_ref[...]
    s1 = jnp.sum(lq1_ref[...] * lk1_ref[...], axis=-1, keepdims=True)
    s2 = jnp.sum(lq2_ref[...] * lk2_ref[...], axis=-1, keepdims=True)
    lam = jnp.exp(s1) - jnp.exp(s2) + lam_init
    lam_ref[...] = jnp.broadcast_to(lam, lam_ref.shape)


def _ada(c, w_ada, b_ada, lq1, lk1, lq2, lk2, lam_init):
    B, D = c.shape
    cols = w_ada.shape[1]
    bc = D
    vec = pl.BlockSpec((1, ATTN_HEAD_DIM), lambda j: (0, 0))
    return pl.pallas_call(
        functools.partial(_ada_kernel, lam_init=lam_init),
        name="ada",
        grid=(cols // bc,),
        in_specs=[
            pl.BlockSpec((B, D), lambda j: (0, 0)),
            pl.BlockSpec((D, bc), lambda j: (0, j)),
            pl.BlockSpec((1, bc), lambda j: (0, j)),
            vec, vec, vec, vec,
        ],
        out_specs=[
            pl.BlockSpec((B, bc), lambda j: (0, j)),
            pl.BlockSpec((V7X_SUBLANES, V7X_LANES), lambda j: (0, 0)),
        ],
        out_shape=[
            jax.ShapeDtypeStruct((B, cols), F32),
            jax.ShapeDtypeStruct((V7X_SUBLANES, V7X_LANES), F32),
        ],
        compiler_params=pltpu.CompilerParams(dimension_semantics=("arbitrary",)),
    )(c, w_ada, b_ada, lq1, lk1, lq2, lk2)


def _group_rms_scale(t, lo):
    sq = t * t
    s1 = jnp.sum(jnp.where(lo, sq, 0.0), axis=-1, keepdims=True)
    s2 = jnp.sum(jnp.where(lo, 0.0, sq), axis=-1, keepdims=True)
    r1 = lax.rsqrt(s1 * (1.0 / ATTN_HEAD_DIM) + EPS)
    r2 = lax.rsqrt(s2 * (1.0 / ATTN_HEAD_DIM) + EPS)
    return jnp.where(lo, r1, r2)


def _inproj_kernel(x_ref, mod_ref, g1_ref, w_ref, qg_ref, kg_ref, hglu_ref, q_ref, k_ref, v_ref):
    x = x_ref[...]
    ms = jnp.mean(x * x, axis=-1, keepdims=True)
    shift = mod_ref[0:1, :]
    scale = mod_ref[1:2, :]
    h = ((x * lax.rsqrt(ms + EPS)) * g1_ref[...]) * (1.0 + scale) + shift
    hb = h.astype(BF16)
    c2 = 2 * CONV_CH
    ag = jnp.dot(hb, w_ref[:, 0:c2], preferred_element_type=F32)
    hglu_ref[...] = ag[:, :CONV_CH] * _sigmoid(ag[:, CONV_CH:])
    lo = lax.broadcasted_iota(I32, (1, V7X_LANES), 1) < ATTN_HEAD_DIM
    for src_col, g_ref, o_ref in ((c2, qg_ref, q_ref), (c2 + ATTN_WIDTH, kg_ref, k_ref)):
        t = jnp.dot(hb, w_ref[:, src_col:src_col + ATTN_WIDTH], preferred_element_type=F32)
        for hd in range(ATTN_HEADS):
            sl = slice(hd * V7X_LANES, (hd + 1) * V7X_LANES)
            th = t[:, sl]
            o_ref[:, sl] = (th * _group_rms_scale(th, lo) * g_ref[:, sl]).astype(BF16)
    v0 = c2 + 2 * ATTN_WIDTH
    v_ref[...] = jnp.dot(hb, w_ref[:, v0:v0 + ATTN_WIDTH], preferred_element_type=F32).astype(BF16)


def _inproj(x2d, mod3, g1, w_in_b, qg, kg, seq):
    N, D = x2d.shape
    tm = INPROJ_ROWS
    steps_per_seq = seq // tm
    cols = w_in_b.shape[1]
    row = lambda i: (i, 0)
    const = lambda i: (0, 0)
    return pl.pallas_call(
        _inproj_kernel,
        name="inproj",
        grid=(N // tm,),
        in_specs=[
            pl.BlockSpec((tm, D), row),
            pl.BlockSpec((None, N_MOD, D), lambda i: (i // steps_per_seq, 0, 0)),
            pl.BlockSpec((1, D), const),
            pl.BlockSpec((D, cols), const),
            pl.BlockSpec((1, ATTN_WIDTH), const),
            pl.BlockSpec((1, ATTN_WIDTH), const),
        ],
        out_specs=[
            pl.BlockSpec((tm, CONV_CH), row),
            pl.BlockSpec((tm, ATTN_WIDTH), row),
            pl.BlockSpec((tm, ATTN_WIDTH), row),
            pl.BlockSpec((tm, ATTN_WIDTH), row),
        ],
        out_shape=[
            jax.ShapeDtypeStruct((N, CONV_CH), F32),
            jax.ShapeDtypeStruct((N, ATTN_WIDTH), BF16),
            jax.ShapeDtypeStruct((N, ATTN_WIDTH), BF16),
            jax.ShapeDtypeStruct((N, ATTN_WIDTH), BF16),
        ],
        compiler_params=pltpu.CompilerParams(
            dimension_semantics=("parallel",), vmem_limit_bytes=VMEM_LIMIT),
    )(x2d, mod3, g1, w_in_b, qg, kg)


ATTN_FEAT = 9


def _split3_bf16(x):
    def trunc(v):
        return (np.asarray(v, np.float32).view(np.uint32) & np.uint32(0xFFFF0000)).view(np.float32)
    x = np.asarray(x, np.float32)
    x0 = trunc(x)
    x1 = trunc(x - x0)
    x2 = trunc(x - x0 - x1)
    return [x0, x1, x2]


def _alibi_constants(seq):
    slopes = (np.exp2(-8.0 * np.arange(1, ATTN_HEADS + 1) / ATTN_HEADS) * math.log2(math.e)).astype(np.float32)
    pos = np.arange(seq)
    a = (pos // 16).astype(np.float32)
    b = (pos % 16).astype(np.float32)
    one = np.ones((seq,), np.float32)
    qfeat = np.zeros((seq, V7X_LANES), np.float32)
    qfeat[:, :ATTN_FEAT] = np.stack([a, a, a, b, b, b, one, one, one], axis=1)
    kfeat = np.zeros((ATTN_HEADS, seq, V7X_LANES), np.float32)
    for h in range(ATTN_HEADS):
        sl = slopes[h]
        cols = ([-c * one for c in _split3_bf16(np.float32(16.0) * sl)] + [-c * one for c in _split3_bf16(sl)]
                + _split3_bf16(sl * pos.astype(np.float32)))
        kfeat[h, :, :ATTN_FEAT] = np.stack(cols, axis=1)
    return jnp.asarray(slopes), jnp.asarray(qfeat, BF16), jnp.asarray(kfeat, BF16)


def _attn_kernel(lam_ref, slope_ref, q_ref, qf_ref, k_ref, kf_ref, v_ref, sg_ref, o_ref, *, ts, lam_init):
    hd = pl.program_id(1)
    lam = lam_ref[0]
    slope = slope_ref[hd]
    S = k_ref.shape[0]
    lo = lax.broadcasted_iota(I32, (1, V7X_LANES), 1) < ATTN_HEAD_DIM
    rel = lax.broadcasted_iota(I32, (ts, ts), 0) - lax.broadcasted_iota(I32, (ts, ts), 1)
    dbias = jnp.abs(rel).astype(F32) * slope
    k = k_ref[...]
    kf = kf_ref[...]
    v1 = jnp.concatenate([v_ref[...], jnp.ones((S, ATTN_V_DIM), BF16)], axis=1)
    k_left = jnp.concatenate([k, kf], axis=1)
    k_right = jnp.concatenate([k, -kf], axis=1)
    dn = _nt_dims()

    for t in range(S // ts):
        rows = slice(t * ts, (t + 1) * ts)
        q = q_ref[rows, :]
        qf = qf_ref[rows, :]
        zero = jnp.zeros_like(q)
        outs = []
        for qc in (jnp.where(lo, q, zero), jnp.where(lo, zero, q)):
            qa = jnp.concatenate([qc, qf], axis=1)
            parts = [(lax.dot_general(qc, k_ref[rows, :], dn, preferred_element_type=F32) - dbias, v1[rows])]
            if t > 0:
                parts.append((lax.dot_general(qa, k_left[:t * ts], dn, preferred_element_type=F32),
                              v1[:t * ts]))
            if (t + 1) * ts < S:
                parts.append((lax.dot_general(qa, k_right[(t + 1) * ts:], dn, preferred_element_type=F32),
                              v1[(t + 1) * ts:]))
            m = parts[0][0].max(axis=-1, keepdims=True)
            for sp, _ in parts[1:]:
                m = jnp.maximum(m, sp.max(axis=-1, keepdims=True))
            acc = jnp.zeros((ts, 2 * ATTN_V_DIM), F32)
            for sp, vp in parts:
                acc = acc + jnp.dot(jnp.exp2(sp - m).astype(BF16), vp, preferred_element_type=F32)
            outs.append((acc[:, :ATTN_V_DIM], acc[:, ATTN_V_DIM:ATTN_V_DIM + 1]))
        (a1, l1), (a2, l2) = outs
        o = a1 * (1.0 / l1) - a2 * (lam / l2)
        ms = jnp.mean(o * o, axis=-1, keepdims=True)
        o = (o * lax.rsqrt(ms + EPS)) * sg_ref[...] * (1.0 - lam_init)
        o_ref[rows, :] = o.astype(BF16)


def _attention(lam1, slopes, qfeat, kfeat, q, k, v, sg, batch, seq, lam_init):
    smem = pl.BlockSpec(memory_space=pltpu.SMEM)
    seq_blk = lambda b, h: (b, h)
    return pl.pallas_call(
        functools.partial(_attn_kernel, ts=ATTN_SUB_ROWS, lam_init=lam_init),
        name="attn",
        grid=(batch, ATTN_HEADS),
        in_specs=[
            smem, smem,
            pl.BlockSpec((seq, ATTN_V_DIM), seq_blk),
            pl.BlockSpec((seq, V7X_LANES), lambda b, h: (0, 0)),
            pl.BlockSpec((seq, ATTN_V_DIM), seq_blk),
            pl.BlockSpec((None, seq, V7X_LANES), lambda b, h: (h, 0, 0)),
            pl.BlockSpec((seq, ATTN_V_DIM), seq_blk),
            pl.BlockSpec((1, ATTN_V_DIM), lambda b, h: (0, 0)),
        ],
        out_specs=pl.BlockSpec((seq, ATTN_V_DIM), seq_blk),
        out_shape=jax.ShapeDtypeStruct(q.shape, BF16),
        compiler_params=pltpu.CompilerParams(
            dimension_semantics=("parallel", "parallel"), vmem_limit_bytes=VMEM_LIMIT),
    )(lam1, slopes, q, qfeat, k, kfeat, v, sg)


def _mixout_kernel(prev_ref, main_ref, next_ref, attn_ref, x_ref, mod_ref, cw_ref, cb_ref, lg_ref, lb_ref,
                   wo_ref, x1_ref, win_ref, sh_ref, *, tiles_per_seq):
    T = main_ref.shape[0]
    j = lax.rem(pl.program_id(0), tiles_per_seq)
    H = CONV_HALO
    win_ref[0:H, :] = jnp.where(j == 0, 0.0, prev_ref[...])
    win_ref[H:H + T, :] = main_ref[...]
    win_ref[H + T:H + T + H, :] = jnp.where(j == tiles_per_seq - 1, 0.0, next_ref[...])
    rows = sh_ref.shape[1]
    for r in range(1, V7X_SUBLANES):
        sh_ref[r] = win_ref[r:r + rows, :]
    acc = jnp.broadcast_to(cb_ref[...], (T, CONV_CH))
    off = H - CONV_KERNEL // 2
    for tap in range(CONV_KERNEL):
        a, r = divmod(off + tap, V7X_SUBLANES)
        src = win_ref if r == 0 else sh_ref.at[r]
        acc = acc + cw_ref[tap:tap + 1, :] * src[a * V7X_SUBLANES:a * V7X_SUBLANES + T, :]
    mu = jnp.mean(acc, axis=-1, keepdims=True)
    xc = acc - mu
    var = jnp.mean(xc * xc, axis=-1, keepdims=True)
    y = (xc * lax.rsqrt(var + EPS)) * lg_ref[...] + lb_ref[...]
    y = y * _sigmoid(y)
    mix = jnp.dot(y.astype(BF16), wo_ref[0:CONV_CH, :], preferred_element_type=F32)
    mix = mix + jnp.dot(attn_ref[...], wo_ref[CONV_CH:CONV_CH + ATTN_WIDTH, :], preferred_element_type=F32)
    x1_ref[...] = x_ref[...] + mod_ref[2:3, :] * mix


def _mixout(hglu, attn, x2d, mod3, cw, cb, lg, lb, wo_b, seq):
    N, D = x2d.shape
    T = MIXOUT_ROWS
    tiles_per_seq = seq // T
    hpt = T // CONV_HALO
    n_halo = N // CONV_HALO
    row = lambda i: (i, 0)
    const = lambda i: (0, 0)
    return pl.pallas_call(
        functools.partial(_mixout_kernel, tiles_per_seq=tiles_per_seq),
        name="mixout",
        grid=(N // T,),
        in_specs=[
            pl.BlockSpec((CONV_HALO, CONV_CH), lambda i: (jnp.maximum(i * hpt - 1, 0), 0)),
            pl.BlockSpec((T, CONV_CH), row),
            pl.BlockSpec((CONV_HALO, CONV_CH), lambda i: (jnp.minimum((i + 1) * hpt, n_halo - 1), 0)),
            pl.BlockSpec((T, ATTN_WIDTH), row),
            pl.BlockSpec((T, D), row),
            pl.BlockSpec((None, N_MOD, D), lambda i: (i // tiles_per_seq, 0, 0)),
            pl.BlockSpec(cw.shape, const),
            pl.BlockSpec((1, CONV_CH), const),
            pl.BlockSpec((1, CONV_CH), const),
            pl.BlockSpec((1, CONV_CH), const),
            pl.BlockSpec(wo_b.shape, const),
        ],
        out_specs=pl.BlockSpec((T, D), row),
        out_shape=jax.ShapeDtypeStruct((N, D), F32),
        scratch_shapes=[
            pltpu.VMEM((T + 2 * CONV_HALO, CONV_CH), F32),
            pltpu.VMEM((V7X_SUBLANES, T + 2 * CONV_HALO - V7X_SUBLANES, CONV_CH), F32),
        ],
        compiler_params=pltpu.CompilerParams(
            dimension_semantics=("parallel",), vmem_limit_bytes=VMEM_LIMIT),
    )(hglu, hglu, hglu, attn, x2d, mod3, cw, cb, lg, lb, wo_b)


def _route_kernel(x1_ref, mod_ref, g2_ref, wr_ref, br_ref, xs_ref, rt_ref, ce_ref, *, n_steps):
    i = pl.program_id(0)

    @pl.when(i < n_steps)
    def _():
        _route_tiles(x1_ref, mod_ref, g2_ref, wr_ref, br_ref, xs_ref, rt_ref, ce_ref)

    @pl.when(i >= n_steps)
    def _():
        xs_ref[...] = jnp.zeros(xs_ref.shape, U32)
        rt_ref[...] = jnp.zeros(rt_ref.shape, F32)
        ce_ref[...] = jnp.full(ce_ref.shape, N_EXPERTS, I32)


def _route_tiles(x1_ref, mod_ref, g2_ref, wr_ref, br_ref, xs_ref, rt_ref, ce_ref):
    G, T, E, L = ROUTE_TILES_PER_STEP, TOK_TILE, N_EXPERTS, SORT_ROWS
    W = G * T
    x1 = x1_ref[...]
    ms = jnp.mean(x1 * x1, axis=-1, keepdims=True)
    h2 = ((x1 * lax.rsqrt(ms + EPS)) * g2_ref[...]) * (1.0 + mod_ref[4:5, :]) + mod_ref[3:4, :]
    h2b = h2.astype(BF16)
    logits = lax.dot_general(wr_ref[...], h2, _nt_dims(), precision=HIGHEST,
                             preferred_element_type=F32) + br_ref[...]
    e_iota = lax.broadcasted_iota(I32, (E, W), 0)
    sels, tops = [], []
    l = logits
    for _ in range(TOP_K):
        m = jnp.max(l, axis=0, keepdims=True)
        idx = jnp.min(jnp.where(l == m, e_iota, E), axis=0, keepdims=True)
        sel = e_iota == idx
        l = jnp.where(sel, -jnp.inf, l)
        sels.append(sel)
        tops.append(m)
    ws = [jnp.exp(m - tops[0]) for m in tops]
    den = ws[0] + ws[1] + ws[2] + ws[3]
    gates = [w / den for w in ws]
    multi = jnp.zeros((E, W), F32)
    for sel in sels:
        multi = multi + jnp.where(sel, 1.0, 0.0)
    multi_b = multi.astype(BF16)
    r_iota = lax.broadcasted_iota(I32, (T, T + V7X_LANES), 0)
    c_iota = lax.broadcasted_iota(I32, (T, T + V7X_LANES), 1)
    tri = jnp.where((r_iota < c_iota) | (c_iota >= T), 1.0, 0.0).astype(BF16)
    rks = [jnp.dot(multi_b[:, h * T:(h + 1) * T], tri, preferred_element_type=F32) for h in range(G)]
    cnt = jnp.concatenate([rk[:, T:T + V7X_LANES] for rk in rks], axis=1)
    n8 = jnp.floor((cnt + (CHUNK - 1)) * (1.0 / CHUNK))
    e_iota_l = lax.broadcasted_iota(I32, n8.shape, 0)
    lo8 = jnp.zeros(n8.shape, F32)
    for e in range(E - 1):
        lo8 = lo8 + jnp.where(e_iota_l > e, n8[e:e + 1, :], 0.0)
    s_iota = lax.broadcasted_iota(I32, (E, ce_ref.shape[2]), 1).astype(F32)
    j_iota = lax.broadcasted_iota(I32, (L, T), 0).astype(F32)
    for h in range(G):
        cols = slice(h * T, (h + 1) * T)
        lo_h = lo8[:, h * V7X_LANES:h * V7X_LANES + 1]
        n_h = n8[:, h * V7X_LANES:h * V7X_LANES + 1]
        base = lo_h * float(CHUNK) + rks[h][:, :T]
        dests = [jnp.sum(jnp.where(sel[:, cols], base, 0.0), axis=0, keepdims=True) for sel in sels]
        for k in range(TOP_K):
            rt_ref[h, k:k + 1, :] = dests[k]
            rt_ref[h, TOP_K + k:TOP_K + k + 1, :] = gates[k][:, cols]
        ce_ref[h] = jnp.sum(jnp.where(lo_h + n_h <= s_iota, 1.0, 0.0), axis=0, keepdims=True).astype(I32)
        pm = jnp.zeros((L, T), F32)
        for d in dests:
            pm = pm + jnp.where(j_iota == d, 1.0, 0.0)
        xs_ref[h * L:(h + 1) * L, :] = _pack_halves(
            jnp.dot(pm.astype(BF16), h2b[h * T:(h + 1) * T, :], preferred_element_type=F32), is_bf16_exact=True)


def _route(x1, mod3, g2, wr_t, br, seq, trash_tiles):
    N, D = x1.shape
    T = TOK_TILE
    tiles_per_seq = seq // T
    n_tiles = N // T
    n_ext = n_tiles + trash_tiles
    L = SORT_ROWS
    G = ROUTE_TILES_PER_STEP
    assert tiles_per_seq % G == 0 and n_ext % G == 0
    n_steps = n_tiles // G
    ce_w = _round_up(SORT_CHUNKS, V7X_LANES)
    const = lambda i: (0, 0)
    real = lambda i: jnp.minimum(i, n_steps - 1)
    return pl.pallas_call(
        functools.partial(_route_kernel, n_steps=n_steps),
        name="route",
        grid=(n_ext // G,),
        in_specs=[
            pl.BlockSpec((G * T, D), lambda i: (real(i), 0)),
            pl.BlockSpec((None, N_MOD, D), lambda i: (real(i) // (tiles_per_seq // G), 0, 0)),
            pl.BlockSpec((1, D), const),
            pl.BlockSpec(wr_t.shape, const),
            pl.BlockSpec(br.shape, const),
        ],
        out_specs=[
            pl.BlockSpec((G * L, D // 2), lambda i: (i, 0)),
            pl.BlockSpec((G, 2 * TOP_K, T), lambda i: (i, 0, 0)),
            pl.BlockSpec((G, 1, ce_w), lambda i: (i, 0, 0)),
        ],
        out_shape=[
            jax.ShapeDtypeStruct((n_ext * L, D // 2), U32),
            jax.ShapeDtypeStruct((n_ext, 2 * TOP_K, T), F32),
            jax.ShapeDtypeStruct((n_ext, 1, ce_w), I32),
        ],
        compiler_params=pltpu.CompilerParams(
            dimension_semantics=("parallel",), vmem_limit_bytes=VMEM_LIMIT),
    )(x1, mod3, g2, wr_t, br)


def _expert_kernel(be_ref, nb_ref, slot_ref, xs_hbm, wgu_ref, bgu_ref, wd_ref, bd_ref, ys_hbm,
                   xbuf, obuf, wgu_b, wd_b, sem_in, sem_out):
    b = pl.program_id(0)
    nb = nb_ref[0]
    dff = wd_ref.shape[0]

    def gather_copy(blk, slot, j):
        chunk = slot_ref[blk * CHUNKS_PER_BLOCK + j]
        return pltpu.make_async_copy(xs_hbm.at[chunk], xbuf.at[slot, pl.ds(j * CHUNK, CHUNK)], sem_in.at[slot])

    def scatter_copy(blk, slot, j):
        chunk = slot_ref[blk * CHUNKS_PER_BLOCK + j]
        return pltpu.make_async_copy(obuf.at[slot, pl.ds(j * CHUNK, CHUNK)], ys_hbm.at[chunk], sem_out.at[slot])

    def start_gather(blk, slot):
        for j in range(CHUNKS_PER_BLOCK):
            gather_copy(blk, slot, j).start()

    def wait_gather(blk, slot):
        for j in range(CHUNKS_PER_BLOCK):
            gather_copy(blk, slot, j).wait()

    def start_scatter(blk, slot):
        for j in range(CHUNKS_PER_BLOCK):
            scatter_copy(blk, slot, j).start()

    def wait_scatter(blk, slot):
        for j in range(CHUNKS_PER_BLOCK):
            scatter_copy(blk, slot, j).wait()

    @pl.when(b < nb)
    def _():
        slot = lax.rem(b, 2)

        @pl.when(b == 0)
        def _():
            start_gather(0, 0)

        @pl.when(b + 1 < nb)
        def _():
            start_gather(b + 1, 1 - slot)

        @pl.when((b == 0) | (be_ref[b] != be_ref[jnp.maximum(b - 1, 0)]))
        def _():
            wgu_b[...] = wgu_ref[...].astype(BF16)
            wd_b[...] = wd_ref[...].astype(BF16)

        wait_gather(b, slot)

        @pl.when(b >= 2)
        def _():
            wait_scatter(b - 2, slot)

        xb = _unpack_halves(xbuf[slot])
        gu = jnp.dot(xb, wgu_b[...], preferred_element_type=F32) + bgu_ref[...]
        g = jnp.minimum(gu[:, :dff], SWIGLU_LIMIT)
        u = jnp.clip(gu[:, dff:], -SWIGLU_LIMIT, SWIGLU_LIMIT)
        y = (u + 1.0) * (g * _sigmoid(SWIGLU_ALPHA * g))
        out = jnp.dot(y.astype(BF16), wd_b[...], preferred_element_type=F32) + bd_ref[...]
        obuf[slot] = _pack_halves(out, is_bf16_exact=False)
        start_scatter(b, slot)

        @pl.when(b == nb - 1)
        def _():
            wait_scatter(b, slot)

            @pl.when(b >= 1)
            def _():
                wait_scatter(b - 1, 1 - slot)


def _experts(blk_expert, nb_total, slots, xs, wgu, bgu, wd, bd):
    nb_max = blk_expert.shape[0]
    half = xs.shape[1]
    D = 2 * half
    f2 = wgu.shape[2]
    dff = wd.shape[1]
    chunks = xs.reshape(xs.shape[0] // CHUNK, CHUNK, half)
    wmap = lambda b, be, nb, sl: (be[b], 0, 0)
    grid_spec = pltpu.PrefetchScalarGridSpec(
        num_scalar_prefetch=3,
        grid=(nb_max,),
        in_specs=[
            pl.BlockSpec(memory_space=pl.ANY),
            pl.BlockSpec((None, D, f2), wmap),
            pl.BlockSpec((None, 1, f2), wmap),
            pl.BlockSpec((None, dff, D), wmap),
            pl.BlockSpec((None, 1, D), wmap),
        ],
        out_specs=pl.BlockSpec(memory_space=pl.ANY),
        scratch_shapes=[
            pltpu.VMEM((2, BLOCK_ROWS, half), U32),
            pltpu.VMEM((2, BLOCK_ROWS, half), U32),
            pltpu.VMEM((D, f2), BF16),
            pltpu.VMEM((dff, D), BF16),
            pltpu.SemaphoreType.DMA((2,)),
            pltpu.SemaphoreType.DMA((2,)),
        ],
    )
    ys = pl.pallas_call(
        _expert_kernel,
        name="experts",
        grid_spec=grid_spec,
        out_shape=jax.ShapeDtypeStruct(chunks.shape, U32),
        input_output_aliases={3: 0},
        compiler_params=pltpu.CompilerParams(
            dimension_semantics=("arbitrary",), vmem_limit_bytes=EXPERT_VMEM_LIMIT),
    )(blk_expert, nb_total, slots, chunks, wgu, bgu, wd, bd)
    return ys.reshape(xs.shape)


def _combine_kernel(ys_ref, rt_ref, x1_ref, mod_ref, x2_ref):
    T, L = TOK_TILE, SORT_ROWS
    j_iota = lax.broadcasted_iota(I32, (T, L), 1).astype(F32)
    for h in range(COMBINE_TILES_PER_STEP):
        rows = slice(h * T, (h + 1) * T)
        wm = jnp.zeros((T, L), F32)
        for k in range(TOP_K):
            wm = wm + jnp.where(j_iota == rt_ref[rows, k:k + 1], rt_ref[rows, TOP_K + k:TOP_K + k + 1], 0.0)
        y = jnp.dot(wm.astype(BF16), _unpack_halves(ys_ref[h * L:(h + 1) * L, :]), preferred_element_type=F32)
        x2_ref[rows, :] = x1_ref[rows, :] + mod_ref[5:6, :] * y


def _combine(ys, rt, x1, mod3, seq):
    N, D = x1.shape
    G = COMBINE_TILES_PER_STEP
    T = G * TOK_TILE
    tiles_per_seq = seq // T
    L = G * SORT_ROWS
    assert seq % T == 0
    return pl.pallas_call(
        _combine_kernel,
        name="combine",
        grid=(N // T,),
        in_specs=[
            pl.BlockSpec((L, D // 2), lambda i: (i, 0)),
            pl.BlockSpec((T, 2 * TOP_K), lambda i: (i, 0)),
            pl.BlockSpec((T, D), lambda i: (i, 0)),
            pl.BlockSpec((None, N_MOD, D), lambda i: (i // tiles_per_seq, 0, 0)),
        ],
        out_specs=pl.BlockSpec((T, D), lambda i: (i, 0)),
        out_shape=jax.ShapeDtypeStruct((N, D), F32),
        compiler_params=pltpu.CompilerParams(
            dimension_semantics=("parallel",), vmem_limit_bytes=VMEM_LIMIT),
    )(ys, rt, x1, mod3)


def _block_tables(chunk_expert, n_tiles):
    E, cpb = N_EXPERTS, CHUNKS_PER_BLOCK
    e_ids = jnp.arange(E, dtype=I32)
    n8 = jnp.sum((chunk_expert[:, 0, :SORT_CHUNKS, None] == e_ids).astype(I32), axis=1)
    lo8 = jnp.cumsum(n8, axis=1) - n8
    excl = jnp.cumsum(n8, axis=0) - n8
    cc = jnp.sum(n8, axis=0)
    nbk = (cc + cpb - 1) // cpb
    bend = jnp.cumsum(nbk)
    bstart = bend - nbk
    nb_total = bend[-1:]
    nb_max = -(-(n_tiles * USED_CHUNKS_MAX) // cpb) + E
    bidx = jnp.arange(nb_max, dtype=I32)
    be = jnp.minimum(jnp.sum((bidx[:, None] >= bend[None, :]).astype(I32), axis=1), E - 1)
    j = jnp.arange(cpb, dtype=I32)[None, :]
    sel = be[:, None] == e_ids[None, :]
    per_block = lambda v: jnp.sum(jnp.where(sel, v[None, :], 0), axis=1)
    pos = (bidx - per_block(bstart))[:, None] * cpb + j
    valid = (pos < per_block(cc)[:, None]) & (bidx < nb_total[0])[:, None]
    base = jnp.arange(n_tiles, dtype=I32)[:, None] * SORT_CHUNKS + lo8 - excl
    picked = jnp.sum(jnp.where(sel[:, None, :], jnp.concatenate([excl, n8, base], axis=0)[None], 0), axis=-1)
    ex_b, n8_b, base_b = jnp.split(picked, 3, axis=1)
    p3 = pos[:, :, None]
    hit = (ex_b[:, None, :] <= p3) & (p3 < (ex_b + n8_b)[:, None, :])
    src = jnp.sum(jnp.where(hit, base_b[:, None, :] + p3, 0), axis=-1)
    trash = be[:, None] * cpb + j
    slots = jnp.where(valid, src, n_tiles * SORT_CHUNKS + trash).astype(I32)
    return be, nb_total.astype(I32), slots.reshape(-1)


def kernel(x, c, w_ada, b_ada, norm1_g, w_in, q_norm_g, k_norm_g, lambda_q1, lambda_k1, lambda_q2,
           lambda_k2, subln_g, conv_w, conv_b, conv_ln_g, conv_ln_b, w_out, norm2_g, w_router, b_router,
           w_gate_up, b_gate_up, w_down, b_down):
    B, S, D = x.shape
    N = B * S
    assert S % INPROJ_ROWS == 0 and S % ATTN_SUB_ROWS == 0 and S % TOK_TILE == 0
    n_tiles = N // TOK_TILE
    trash_tiles = _round_up(-(-(N_EXPERTS * CHUNKS_PER_BLOCK) // SORT_CHUNKS), ROUTE_TILES_PER_STEP)
    log2e = math.log2(math.e)
    slopes, qfeat, kfeat = _alibi_constants(S)
    xt = x.reshape(N, D)
    for l in range(w_ada.shape[0]):
        lam_init = 0.8 - 0.6 * math.exp(-0.3 * l)
        row = lambda a: a[l][None, :]
        mod, lam = _ada(c, w_ada[l], row(b_ada), row(lambda_q1), row(lambda_k1), row(lambda_q2),
                        row(lambda_k2), lam_init)
        mod3 = mod.reshape(B, N_MOD, D)
        qg = jnp.tile(q_norm_g[l], 2 * ATTN_HEADS)[None, :] * (ATTN_HEAD_DIM ** -0.5 * log2e)
        kg = jnp.tile(k_norm_g[l], 2 * ATTN_HEADS)[None, :]
        hglu, q, k, v = _inproj(xt, mod3, row(norm1_g), w_in[l].astype(BF16), qg, kg, S)
        attn = _attention(lam[0, 0:1], slopes, qfeat, kfeat, q, k, v, row(subln_g), B, S, lam_init)
        cw = jnp.pad(conv_w[l], ((0, 1), (0, 0)))
        x1 = _mixout(hglu, attn, xt, mod3, cw, row(conv_b), row(conv_ln_g), row(conv_ln_b),
                     w_out[l].astype(BF16), S)
        xs, rt, ce = _route(x1, mod3, row(norm2_g), w_router[l].T, b_router[l][:, None], S, trash_tiles)
        be, nb_total, slots = _block_tables(ce[:n_tiles], n_tiles)
        ys = _experts(be, nb_total, slots, xs, w_gate_up[l], b_gate_up[l][:, None, :],
                      w_down[l], b_down[l][:, None, :])
        rt_tok = rt[:n_tiles].transpose(0, 2, 1).reshape(N, 2 * TOP_K)
        xt = _combine(ys, rt_tok, x1, mod3, S)
    return xt.reshape(B, S, D)
```

```python
import functools
import math

import jax
import jax.numpy as jnp
import numpy as np
from jax import lax
from jax.experimental import pallas as pl
from jax.experimental.pallas import tpu as pltpu

F32 = jnp.float32
BF16 = jnp.bfloat16
I32 = jnp.int32
U32 = jnp.uint32
HIGHEST = lax.Precision.HIGHEST

CONV_CH = 512
CONV_KERNEL = 31
CONV_HALO = 16
ATTN_HEADS = 4
ATTN_HEAD_DIM = 64
ATTN_V_DIM = 2 * ATTN_HEAD_DIM
ATTN_WIDTH = ATTN_HEADS * ATTN_V_DIM
N_EXPERTS = 32
TOP_K = 4
SWIGLU_LIMIT = 7.0
SWIGLU_ALPHA = 1.702
N_MOD = 6
EPS = 1e-5

V7X_LANES = 128
V7X_SUBLANES = 8
V7X_VMEM_BYTES = 64 * 1024 * 1024
VMEM_LIMIT = V7X_VMEM_BYTES * 3 // 4
EXPERT_VMEM_LIMIT = V7X_VMEM_BYTES * 7 // 8

INPROJ_ROWS = 1024
ATTN_SUB_ROWS = 256
MIXOUT_ROWS = 1024
TOK_TILE = 256
COMBINE_TILES_PER_STEP = 4
ROUTE_TILES_PER_STEP = 4
CHUNK = V7X_SUBLANES
BLOCK_ROWS = 512
CHUNKS_PER_BLOCK = BLOCK_ROWS // CHUNK


def _round_up(a, b):
    return (a + b - 1) // b * b


USED_CHUNKS_MAX = (TOP_K * TOK_TILE + N_EXPERTS * (CHUNK - 1)) // CHUNK
SORT_ROWS = _round_up(USED_CHUNKS_MAX * CHUNK, V7X_LANES)
SORT_CHUNKS = SORT_ROWS // CHUNK


def _sigmoid(v):
    return 1.0 / (1.0 + jnp.exp(-v))


def _nt_dims():
    return (((1,), (1,)), ((), ()))


def _pack_halves(v, *, is_bf16_exact):
    c = v.shape[1] // 2
    bits = lax.bitcast_convert_type(v, U32)
    if is_bf16_exact:
        return (bits[:, :c] >> 16) | bits[:, c:]
    bits = bits + jnp.uint32(0x7FFF) + ((bits >> 16) & jnp.uint32(1))
    return (bits[:, :c] >> 16) | (bits[:, c:] & jnp.uint32(0xFFFF0000))


def _unpack_halves(u):
    lo = lax.bitcast_convert_type(u << 16, F32).astype(BF16)
    hi = lax.bitcast_convert_type(u & jnp.uint32(0xFFFF0000), F32).astype(BF16)
    return jnp.concatenate([lo, hi], axis=1)


def _ada_kernel(c_ref, w_ref, b_ref, lq1_ref, lk1_ref, lq2_ref, lk2_ref, mod_ref, lam_ref, *, lam_init):
    c = c_ref[...]
    sc = c * _sigmoid(c)
    mod_ref[...] = jnp.dot(sc, w_ref[...], precision=HIGHEST, preferred_element_type=F32) + b_ref[...]
    s1 = jnp.sum(lq1_ref[...] * lk1_ref[...], axis=-1, keepdims=True)
    s2 = jnp.sum(lq2_ref[...] * lk2_ref[...], axis=-1, keepdims=True)
    lam = jnp.exp(s1) - jnp.exp(s2) + lam_init
    lam_ref[...] = jnp.broadcast_to(lam, lam_ref.shape)


def _ada(c, w_ada, b_ada, lq1, lk1, lq2, lk2, lam_init):
    B, D = c.shape
    cols = w_ada.shape[1]
    bc = D
    vec = pl.BlockSpec((1, ATTN_HEAD_DIM), lambda j: (0, 0))
    return pl.pallas_call(
        functools.partial(_ada_kernel, lam_init=lam_init),
        name="ada",
        grid=(cols // bc,),
        in_specs=[
            pl.BlockSpec((B, D), lambda j: (0, 0)),
            pl.BlockSpec((D, bc), lambda j: (0, j)),
            pl.BlockSpec((1, bc), lambda j: (0, j)),
            vec, vec, vec, vec,
        ],
        out_specs=[
            pl.BlockSpec((B, bc), lambda j: (0, j)),
            pl.BlockSpec((V7X_SUBLANES, V7X_LANES), lambda j: (0, 0)),
        ],
        out_shape=[
            jax.ShapeDtypeStruct((B, cols), F32),
            jax.ShapeDtypeStruct((V7X_SUBLANES, V7X_LANES), F32),
        ],
        compiler_params=pltpu.CompilerParams(dimension_semantics=("arbitrary",)),
    )(c, w_ada, b_ada, lq1, lk1, lq2, lk2)


def _group_rms_scale(t, lo):
    sq = t * t
    s1 = jnp.sum(jnp.where(lo, sq, 0.0), axis=-1, keepdims=True)
    s2 = jnp.sum(jnp.where(lo, 0.0, sq), axis=-1, keepdims=True)
    r1 = lax.rsqrt(s1 * (1.0 / ATTN_HEAD_DIM) + EPS)
    r2 = lax.rsqrt(s2 * (1.0 / ATTN_HEAD_DIM) + EPS)
    return jnp.where(lo, r1, r2)


def _inproj_kernel(x_ref, mod_ref, g1_ref, w_ref, qg_ref, kg_ref, hglu_ref, q_ref, k_ref, v_ref):
    x = x_ref[...]
    ms = jnp.mean(x * x, axis=-1, keepdims=True)
    shift = mod_ref[0:1, :]
    scale = mod_ref[1:2, :]
    h = ((x * lax.rsqrt(ms + EPS)) * g1_ref[...]) * (1.0 + scale) + shift
    hb = h.astype(BF16)
    c2 = 2 * CONV_CH
    ag = jnp.dot(hb, w_ref[:, 0:c2], preferred_element_type=F32)
    hglu_ref[...] = ag[:, :CONV_CH] * _sigmoid(ag[:, CONV_CH:])
    lo = lax.broadcasted_iota(I32, (1, V7X_LANES), 1) < ATTN_HEAD_DIM
    for src_col, g_ref, o_ref in ((c2, qg_ref, q_ref), (c2 + ATTN_WIDTH, kg_ref, k_ref)):
        t = jnp.dot(hb, w_ref[:, src_col:src_col + ATTN_WIDTH], preferred_element_type=F32)
        for hd in range(ATTN_HEADS):
            sl = slice(hd * V7X_LANES, (hd + 1) * V7X_LANES)
            th = t[:, sl]
            o_ref[:, sl] = (th * _group_rms_scale(th, lo) * g_ref[:, sl]).astype(BF16)
    v0 = c2 + 2 * ATTN_WIDTH
    v_ref[...] = jnp.dot(hb, w_ref[:, v0:v0 + ATTN_WIDTH], preferred_element_type=F32).astype(BF16)


def _inproj(x2d, mod3, g1, w_in_b, qg, kg, seq):
    N, D = x2d.shape
    tm = INPROJ_ROWS
    steps_per_seq = seq // tm
    cols = w_in_b.shape[1]
    row = lambda i: (i, 0)
    const = lambda i: (0, 0)
    return pl.pallas_call(
        _inproj_kernel,
        name="inproj",
        grid=(N // tm,),
        in_specs=[
            pl.BlockSpec((tm, D), row),
            pl.BlockSpec((None, N_MOD, D), lambda i: (i // steps_per_seq, 0, 0)),
            pl.BlockSpec((1, D), const),
            pl.BlockSpec((D, cols), const),
            pl.BlockSpec((1, ATTN_WIDTH), const),
            pl.BlockSpec((1, ATTN_WIDTH), const),
        ],
        out_specs=[
            pl.BlockSpec((tm, CONV_CH), row),
            pl.BlockSpec((tm, ATTN_WIDTH), row),
            pl.BlockSpec((tm, ATTN_WIDTH), row),
            pl.BlockSpec((tm, ATTN_WIDTH), row),
        ],
        out_shape=[
            jax.ShapeDtypeStruct((N, CONV_CH), F32),
            jax.ShapeDtypeStruct((N, ATTN_WIDTH), BF16),
            jax.ShapeDtypeStruct((N, ATTN_WIDTH), BF16),
            jax.ShapeDtypeStruct((N, ATTN_WIDTH), BF16),
        ],
        compiler_params=pltpu.CompilerParams(
            dimension_semantics=("parallel",), vmem_limit_bytes=VMEM_LIMIT),
    )(x2d, mod3, g1, w_in_b, qg, kg)


ATTN_FEAT = 9


def _split3_bf16(x):
    def trunc(v):
        return (np.asarray(v, np.float32).view(np.uint32) & np.uint32(0xFFFF0000)).view(np.float32)
    x = np.asarray(x, np.float32)
    x0 = trunc(x)
    x1 = trunc(x - x0)
    x2 = trunc(x - x0 - x1)
    return [x0, x1, x2]


def _alibi_constants(seq):
    slopes = (np.exp2(-8.0 * np.arange(1, ATTN_HEADS + 1) / ATTN_HEADS) * math.log2(math.e)).astype(np.float32)
    pos = np.arange(seq)
    a = (pos // 16).astype(np.float32)
    b = (pos % 16).astype(np.float32)
    one = np.ones((seq,), np.float32)
    qfeat = np.zeros((seq, V7X_LANES), np.float32)
    qfeat[:, :ATTN_FEAT] = np.stack([a, a, a, b, b, b, one, one, one], axis=1)
    kfeat = np.zeros((ATTN_HEADS, seq, V7X_LANES), np.float32)
    for h in range(ATTN_HEADS):
        sl = slopes[h]
        cols = ([-c * one for c in _split3_bf16(np.float32(16.0) * sl)] + [-c * one for c in _split3_bf16(sl)]
                + _split3_bf16(sl * pos.astype(np.float32)))
        kfeat[h, :, :ATTN_FEAT] = np.stack(cols, axis=1)
    return jnp.asarray(slopes), jnp.asarray(qfeat, BF16), jnp.asarray(kfeat, BF16)


def _attn_kernel(lam_ref, slope_ref, q_ref, qf_ref, k_ref, kf_ref, v_ref, sg_ref, o_ref, *, ts, lam_init):
    hd = pl.program_id(1)
    lam = lam_ref[0]
    slope = slope_ref[hd]
    S = k_ref.shape[0]
    lo = lax.broadcasted_iota(I32, (1, V7X_LANES), 1) < ATTN_HEAD_DIM
    rel = lax.broadcasted_iota(I32, (ts, ts), 0) - lax.broadcasted_iota(I32, (ts, ts), 1)
    dbias = jnp.abs(rel).astype(F32) * slope
    k = k_ref[...]
    kf = kf_ref[...]
    v1 = jnp.concatenate([v_ref[...], jnp.ones((S, ATTN_V_DIM), BF16)], axis=1)
    k_left = jnp.concatenate([k, kf], axis=1)
    k_right = jnp.concatenate([k, -kf], axis=1)
    dn = _nt_dims()

    for t in range(S // ts):
        rows = slice(t * ts, (t + 1) * ts)
        q = q_ref[rows, :]
        qf = qf_ref[rows, :]
        zero = jnp.zeros_like(q)
        outs = []
        for qc in (jnp.where(lo, q, zero), jnp.where(lo, zero, q)):
            qa = jnp.concatenate([qc, qf], axis=1)
            parts = [(lax.dot_general(qc, k_ref[rows, :], dn, preferred_element_type=F32) - dbias, v1[rows])]
            if t > 0:
                parts.append((lax.dot_general(qa, k_left[:t * ts], dn, preferred_element_type=F32),
                              v1[:t * ts]))
            if (t + 1) * ts < S:
                parts.append((lax.dot_general(qa, k_right[(t + 1) * ts:], dn, preferred_element_type=F32),
                              v1[(t + 1) * ts:]))
            m = parts[0][0].max(axis=-1, keepdims=True)
            for sp, _ in parts[1:]:
                m = jnp.maximum(m, sp.max(axis=-1, keepdims=True))
            acc = jnp.zeros((ts, 2 * ATTN_V_DIM), F32)
            for sp, vp in parts:
                acc = acc + jnp.dot(jnp.exp2(sp - m).astype(BF16), vp, preferred_element_type=F32)
            outs.append((acc[:, :ATTN_V_DIM], acc[:, ATTN_V_DIM:ATTN_V_DIM + 1]))
        (a1, l1), (a2, l2) = outs
        o = a1 * (1.0 / l1) - a2 * (lam / l2)
        ms = jnp.mean(o * o, axis=-1, keepdims=True)
        o = (o * lax.rsqrt(ms + EPS)) * sg_ref[...] * (1.0 - lam_init)
        o_ref[rows, :] = o.astype(BF16)


def _attention(lam1, slopes, qfeat, kfeat, q, k, v, sg, batch, seq, lam_init):
    smem = pl.BlockSpec(memory_space=pltpu.SMEM)
    seq_blk = lambda b, h: (b, h)
    return pl.pallas_call(
        functools.partial(_attn_kernel, ts=ATTN_SUB_ROWS, lam_init=lam_init),
        name="attn",
        grid=(batch, ATTN_HEADS),
        in_specs=[
            smem, smem,
            pl.BlockSpec((seq, ATTN_V_DIM), seq_blk),
            pl.BlockSpec((seq, V7X_LANES), lambda b, h: (0, 0)),
            pl.BlockSpec((seq, ATTN_V_DIM), seq_blk),
            pl.BlockSpec((None, seq, V7X_LANES), lambda b, h: (h, 0, 0)),
            pl.BlockSpec((seq, ATTN_V_DIM), seq_blk),
            pl.BlockSpec((1, ATTN_V_DIM), lambda b, h: (0, 0)),
        ],
        out_specs=pl.BlockSpec((seq, ATTN_V_DIM), seq_blk),
        out_shape=jax.ShapeDtypeStruct(q.shape, BF16),
        compiler_params=pltpu.CompilerParams(
            dimension_semantics=("parallel", "parallel"), vmem_limit_bytes=VMEM_LIMIT),
    )(lam1, slopes, q, qfeat, k, kfeat, v, sg)


def _mixout_kernel(prev_ref, main_ref, next_ref, attn_ref, x_ref, mod_ref, cw_ref, cb_ref, lg_ref, lb_ref,
                   wo_ref, x1_ref, win_ref, sh_ref, *, tiles_per_seq):
    T = main_ref.shape[0]
    j = lax.rem(pl.program_id(0), tiles_per_seq)
    H = CONV_HALO
    win_ref[0:H, :] = jnp.where(j == 0, 0.0, prev_ref[...])
    win_ref[H:H + T, :] = main_ref[...]
    win_ref[H + T:H + T + H, :] = jnp.where(j == tiles_per_seq - 1, 0.0, next_ref[...])
    rows = sh_ref.shape[1]
    for r in range(1, V7X_SUBLANES):
        sh_ref[r] = win_ref[r:r + rows, :]
    acc = jnp.broadcast_to(cb_ref[...], (T, CONV_CH))
    off = H - CONV_KERNEL // 2
    for tap in range(CONV_KERNEL):
        a, r = divmod(off + tap, V7X_SUBLANES)
        src = win_ref if r == 0 else sh_ref.at[r]
        acc = acc + cw_ref[tap:tap + 1, :] * src[a * V7X_SUBLANES:a * V7X_SUBLANES + T, :]
    mu = jnp.mean(acc, axis=-1, keepdims=True)
    xc = acc - mu
    var = jnp.mean(xc * xc, axis=-1, keepdims=True)
    y = (xc * lax.rsqrt(var + EPS)) * lg_ref[...] + lb_ref[...]
    y = y * _sigmoid(y)
    mix = jnp.dot(y.astype(BF16), wo_ref[0:CONV_CH, :], preferred_element_type=F32)
    mix = mix + jnp.dot(attn_ref[...], wo_ref[CONV_CH:CONV_CH + ATTN_WIDTH, :], preferred_element_type=F32)
    x1_ref[...] = x_ref[...] + mod_ref[2:3, :] * mix


def _mixout(hglu, attn, x2d, mod3, cw, cb, lg, lb, wo_b, seq):
    N, D = x2d.shape
    T = MIXOUT_ROWS
    tiles_per_seq = seq // T
    hpt = T // CONV_HALO
    n_halo = N // CONV_HALO
    row = lambda i: (i, 0)
    const = lambda i: (0, 0)
    return pl.pallas_call(
        functools.partial(_mixout_kernel, tiles_per_seq=tiles_per_seq),
        name="mixout",
        grid=(N // T,),
        in_specs=[
            pl.BlockSpec((CONV_HALO, CONV_CH), lambda i: (jnp.maximum(i * hpt - 1, 0), 0)),
            pl.BlockSpec((T, CONV_CH), row),
            pl.BlockSpec((CONV_HALO, CONV_CH), lambda i: (jnp.minimum((i + 1) * hpt, n_halo - 1), 0)),
            pl.BlockSpec((T, ATTN_WIDTH), row),
            pl.BlockSpec((T, D), row),
            pl.BlockSpec((None, N_MOD, D), lambda i: (i // tiles_per_seq, 0, 0)),
            pl.BlockSpec(cw.shape, const),
            pl.BlockSpec((1, CONV_CH), const),
            pl.BlockSpec((1, CONV_CH), const),
            pl.BlockSpec((1, CONV_CH), const),
            pl.BlockSpec(wo_b.shape, const),
        ],
        out_specs=pl.BlockSpec((T, D), row),
        out_shape=jax.ShapeDtypeStruct((N, D), F32),
        scratch_shapes=[
            pltpu.VMEM((T + 2 * CONV_HALO, CONV_CH), F32),
            pltpu.VMEM((V7X_SUBLANES, T + 2 * CONV_HALO - V7X_SUBLANES, CONV_CH), F32),
        ],
        compiler_params=pltpu.CompilerParams(
            dimension_semantics=("parallel",), vmem_limit_bytes=VMEM_LIMIT),
    )(hglu, hglu, hglu, attn, x2d, mod3, cw, cb, lg, lb, wo_b)


def _route_kernel(x1_ref, mod_ref, g2_ref, wr_ref, br_ref, xs_ref, rt_ref, ce_ref, *, n_steps):
    i = pl.program_id(0)

    @pl.when(i < n_steps)
    def _():
        _route_tiles(x1_ref, mod_ref, g2_ref, wr_ref, br_ref, xs_ref, rt_ref, ce_ref)

    @pl.when(i >= n_steps)
    def _():
        xs_ref[...] = jnp.zeros(xs_ref.shape, U32)
        rt_ref[...] = jnp.zeros(rt_ref.shape, F32)
        ce_ref[...] = jnp.full(ce_ref.shape, N_EXPERTS, I32)


def _route_tiles(x1_ref, mod_ref, g2_ref, wr_ref, br_ref, xs_ref, rt_ref, ce_ref):
    G, T, E, L = ROUTE_TILES_PER_STEP, TOK_TILE, N_EXPERTS, SORT_ROWS
    W = G * T
    x1 = x1_ref[...]
    ms = jnp.mean(x1 * x1, axis=-1, keepdims=True)
    h2 = ((x1 * lax.rsqrt(ms + EPS)) * g2_ref[...]) * (1.0 + mod_ref[4:5, :]) + mod_ref[3:4, :]
    h2b = h2.astype(BF16)
    logits = lax.dot_general(wr_ref[...], h2, _nt_dims(), precision=HIGHEST,
                             preferred_element_type=F32) + br_ref[...]
    e_iota = lax.broadcasted_iota(I32, (E, W), 0)
    sels, tops = [], []
    l = logits
    for _ in range(TOP_K):
        m = jnp.max(l, axis=0, keepdims=True)
        idx = jnp.min(jnp.where(l == m, e_iota, E), axis=0, keepdims=True)
        sel = e_iota == idx
        l = jnp.where(sel, -jnp.inf, l)
        sels.append(sel)
        tops.append(m)
    ws = [jnp.exp(m - tops[0]) for m in tops]
    den = ws[0] + ws[1] + ws[2] + ws[3]
    gates = [w / den for w in ws]
    multi = jnp.zeros((E, W), F32)
    for sel in sels:
        multi = multi + jnp.where(sel, 1.0, 0.0)
    multi_b = multi.astype(BF16)
    r_iota = lax.broadcasted_iota(I32, (T, T + V7X_LANES), 0)
    c_iota = lax.broadcasted_iota(I32, (T, T + V7X_LANES), 1)
    tri = jnp.where((r_iota < c_iota) | (c_iota >= T), 1.0, 0.0).astype(BF16)
    rks = [jnp.dot(multi_b[:, h * T:(h + 1) * T], tri, preferred_element_type=F32) for h in range(G)]
    cnt = jnp.concatenate([rk[:, T:T + V7X_LANES] for rk in rks], axis=1)
    n8 = jnp.floor((cnt + (CHUNK - 1)) * (1.0 / CHUNK))
    e_iota_l = lax.broadcasted_iota(I32, n8.shape, 0)
    lo8 = jnp.zeros(n8.shape, F32)
    for e in range(E - 1):
        lo8 = lo8 + jnp.where(e_iota_l > e, n8[e:e + 1, :], 0.0)
    s_iota = lax.broadcasted_iota(I32, (E, ce_ref.shape[2]), 1).astype(F32)
    j_iota = lax.broadcasted_iota(I32, (L, T), 0).astype(F32)
    for h in range(G):
        cols = slice(h * T, (h + 1) * T)
        lo_h = lo8[:, h * V7X_LANES:h * V7X_LANES + 1]
        n_h = n8[:, h * V7X_LANES:h * V7X_LANES + 1]
        base = lo_h * float(CHUNK) + rks[h][:, :T]
        dests = [jnp.sum(jnp.where(sel[:, cols], base, 0.0), axis=0, keepdims=True) for sel in sels]
        for k in range(TOP_K):
            rt_ref[h, k:k + 1, :] = dests[k]
            rt_ref[h, TOP_K + k:TOP_K + k + 1, :] = gates[k][:, cols]
        ce_ref[h] = jnp.sum(jnp.where(lo_h + n_h <= s_iota, 1.0, 0.0), axis=0, keepdims=True).astype(I32)
        pm = jnp.zeros((L, T), F32)
        for d in dests:
            pm = jnp.where(j_iota == d, 1.0, pm)
        xs_ref[h * L:(h + 1) * L, :] = _pack_halves(
            jnp.dot(pm.astype(BF16), h2b[h * T:(h + 1) * T, :], preferred_element_type=F32), is_bf16_exact=True)


def _route(x1, mod3, g2, wr_t, br, seq, trash_tiles):
    N, D = x1.shape
    T = TOK_TILE
    tiles_per_seq = seq // T
    n_tiles = N // T
    n_ext = n_tiles + trash_tiles
    L = SORT_ROWS
    G = ROUTE_TILES_PER_STEP
    assert tiles_per_seq % G == 0 and n_ext % G == 0
    n_steps = n_tiles // G
    ce_w = _round_up(SORT_CHUNKS, V7X_LANES)
    const = lambda i: (0, 0)
    real = lambda i: jnp.minimum(i, n_steps - 1)
    return pl.pallas_call(
        functools.partial(_route_kernel, n_steps=n_steps),
        name="route",
        grid=(n_ext // G,),
        in_specs=[
            pl.BlockSpec((G * T, D), lambda i: (real(i), 0)),
            pl.BlockSpec((None, N_MOD, D), lambda i: (real(i) // (tiles_per_seq // G), 0, 0)),
            pl.BlockSpec((1, D), const),
            pl.BlockSpec(wr_t.shape, const),
            pl.BlockSpec(br.shape, const),
        ],
        out_specs=[
            pl.BlockSpec((G * L, D // 2), lambda i: (i, 0)),
            pl.BlockSpec((G, 2 * TOP_K, T), lambda i: (i, 0, 0)),
            pl.BlockSpec((G, 1, ce_w), lambda i: (i, 0, 0)),
        ],
        out_shape=[
            jax.ShapeDtypeStruct((n_ext * L, D // 2), U32),
            jax.ShapeDtypeStruct((n_ext, 2 * TOP_K, T), F32),
            jax.ShapeDtypeStruct((n_ext, 1, ce_w), I32),
        ],
        compiler_params=pltpu.CompilerParams(
            dimension_semantics=("parallel",), vmem_limit_bytes=VMEM_LIMIT),
    )(x1, mod3, g2, wr_t, br)


def _expert_kernel(be_ref, nb_ref, slot_ref, xs_hbm, wgu_ref, bgu_ref, wd_ref, bd_ref, ys_hbm,
                   xbuf, obuf, wgu_b, wd_b, sem_in, sem_out):
    b = pl.program_id(0)
    nb = nb_ref[0]
    dff = wd_ref.shape[0]

    def gather_copy(blk, slot, j):
        chunk = slot_ref[blk * CHUNKS_PER_BLOCK + j]
        return pltpu.make_async_copy(xs_hbm.at[chunk], xbuf.at[slot, pl.ds(j * CHUNK, CHUNK)], sem_in.at[slot])

    def scatter_copy(blk, slot, j):
        chunk = slot_ref[blk * CHUNKS_PER_BLOCK + j]
        return pltpu.make_async_copy(obuf.at[slot, pl.ds(j * CHUNK, CHUNK)], ys_hbm.at[chunk], sem_out.at[slot])

    def start_gather(blk, slot):
        for j in range(CHUNKS_PER_BLOCK):
            gather_copy(blk, slot, j).start()

    def wait_gather(blk, slot):
        for j in range(CHUNKS_PER_BLOCK):
            gather_copy(blk, slot, j).wait()

    def start_scatter(blk, slot):
        for j in range(CHUNKS_PER_BLOCK):
            scatter_copy(blk, slot, j).start()

    def wait_scatter(blk, slot):
        for j in range(CHUNKS_PER_BLOCK):
            scatter_copy(blk, slot, j).wait()

    @pl.when(b < nb)
    def _():
        slot = lax.rem(b, 2)

        @pl.when(b == 0)
        def _():
            start_gather(0, 0)

        @pl.when(b + 1 < nb)
        def _():
            start_gather(b + 1, 1 - slot)

        @pl.when((b == 0) | (be_ref[b] != be_ref[jnp.maximum(b - 1, 0)]))
        def _():
            wgu_b[...] = wgu_ref[...].astype(BF16)
            wd_b[...] = wd_ref[...].astype(BF16)

        wait_gather(b, slot)

        @pl.when(b >= 2)
        def _():
            wait_scatter(b - 2, slot)

        xb = _unpack_halves(xbuf[slot])
        gu = jnp.dot(xb, wgu_b[...], preferred_element_type=F32) + bgu_ref[...]
        g = jnp.minimum(gu[:, :dff], SWIGLU_LIMIT)
        u = jnp.clip(gu[:, dff:], -SWIGLU_LIMIT, SWIGLU_LIMIT)
        y = (u + 1.0) * (g * _sigmoid(SWIGLU_ALPHA * g))
        out = jnp.dot(y.astype(BF16), wd_b[...], preferred_element_type=F32) + bd_ref[...]
        obuf[slot] = _pack_halves(out, is_bf16_exact=False)
        start_scatter(b, slot)

        @pl.when(b == nb - 1)
        def _():
            wait_scatter(b, slot)

            @pl.when(b >= 1)
            def _():
                wait_scatter(b - 1, 1 - slot)


def _experts(blk_expert, nb_total, slots, xs, wgu, bgu, wd, bd):
    nb_max = blk_expert.shape[0]
    half = xs.shape[1]
    D = 2 * half
    f2 = wgu.shape[2]
    dff = wd.shape[1]
    chunks = xs.reshape(xs.shape[0] // CHUNK, CHUNK, half)
    wmap = lambda b, be, nb, sl: (be[b], 0, 0)
    grid_spec = pltpu.PrefetchScalarGridSpec(
        num_scalar_prefetch=3,
        grid=(nb_max,),
        in_specs=[
            pl.BlockSpec(memory_space=pl.ANY),
            pl.BlockSpec((None, D, f2), wmap),
            pl.BlockSpec((None, 1, f2), wmap),
            pl.BlockSpec((None, dff, D), wmap),
            pl.BlockSpec((None, 1, D), wmap),
        ],
        out_specs=pl.BlockSpec(memory_space=pl.ANY),
        scratch_shapes=[
            pltpu.VMEM((2, BLOCK_ROWS, half), U32),
            pltpu.VMEM((2, BLOCK_ROWS, half), U32),
            pltpu.VMEM((D, f2), BF16),
            pltpu.VMEM((dff, D), BF16),
            pltpu.SemaphoreType.DMA((2,)),
            pltpu.SemaphoreType.DMA((2,)),
        ],
    )
    ys = pl.pallas_call(
        _expert_kernel,
        name="experts",
        grid_spec=grid_spec,
        out_shape=jax.ShapeDtypeStruct(chunks.shape, U32),
        input_output_aliases={3: 0},
        compiler_params=pltpu.CompilerParams(
            dimension_semantics=("arbitrary",), vmem_limit_bytes=EXPERT_VMEM_LIMIT),
    )(blk_expert, nb_total, slots, chunks, wgu, bgu, wd, bd)
    return ys.reshape(xs.shape)


def _combine_kernel(ys_ref, rt_ref, x1_ref, mod_ref, x2_ref):
    T, L = TOK_TILE, SORT_ROWS
    j_iota = lax.broadcasted_iota(I32, (T, L), 1).astype(F32)
    for h in range(COMBINE_TILES_PER_STEP):
        rows = slice(h * T, (h + 1) * T)
        wm = jnp.zeros((T, L), F32)
        for k in range(TOP_K):
            wm = wm + jnp.where(j_iota == rt_ref[rows, k:k + 1], rt_ref[rows, TOP_K + k:TOP_K + k + 1], 0.0)
        y = jnp.dot(wm.astype(BF16), _unpack_halves(ys_ref[h * L:(h + 1) * L, :]), preferred_element_type=F32)
        x2_ref[rows, :] = x1_ref[rows, :] + mod_ref[5:6, :] * y


def _combine(ys, rt, x1, mod3, seq):
    N, D = x1.shape
    G = COMBINE_TILES_PER_STEP
    T = G * TOK_TILE
    tiles_per_seq = seq // T
    L = G * SORT_ROWS
    assert seq % T == 0
    return pl.pallas_call(
        _combine_kernel,
        name="combine",
        grid=(N // T,),
        in_specs=[
            pl.BlockSpec((L, D // 2), lambda i: (i, 0)),
            pl.BlockSpec((T, 2 * TOP_K), lambda i: (i, 0)),
            pl.BlockSpec((T, D), lambda i: (i, 0)),
            pl.BlockSpec((None, N_MOD, D), lambda i: (i // tiles_per_seq, 0, 0)),
        ],
        out_specs=pl.BlockSpec((T, D), lambda i: (i, 0)),
        out_shape=jax.ShapeDtypeStruct((N, D), F32),
        compiler_params=pltpu.CompilerParams(
            dimension_semantics=("parallel",), vmem_limit_bytes=VMEM_LIMIT),
    )(ys, rt, x1, mod3)


def _block_tables(chunk_expert, n_tiles):
    E, cpb = N_EXPERTS, CHUNKS_PER_BLOCK
    e_ids = jnp.arange(E, dtype=I32)
    n8 = jnp.sum((chunk_expert[:, 0, :SORT_CHUNKS, None] == e_ids).astype(I32), axis=1)
    lo8 = jnp.cumsum(n8, axis=1) - n8
    excl = jnp.cumsum(n8, axis=0) - n8
    cc = jnp.sum(n8, axis=0)
    nbk = (cc + cpb - 1) // cpb
    bend = jnp.cumsum(nbk)
    bstart = bend - nbk
    nb_total = bend[-1:]
    nb_max = -(-(n_tiles * USED_CHUNKS_MAX) // cpb) + E
    bidx = jnp.arange(nb_max, dtype=I32)
    be = jnp.minimum(jnp.sum((bidx[:, None] >= bend[None, :]).astype(I32), axis=1), E - 1)
    j = jnp.arange(cpb, dtype=I32)[None, :]
    sel = be[:, None] == e_ids[None, :]
    per_block = lambda v: jnp.sum(jnp.where(sel, v[None, :], 0), axis=1)
    pos = (bidx - per_block(bstart))[:, None] * cpb + j
    valid = (pos < per_block(cc)[:, None]) & (bidx < nb_total[0])[:, None]
    base = jnp.arange(n_tiles, dtype=I32)[:, None] * SORT_CHUNKS + lo8 - excl
    picked = jnp.sum(jnp.where(sel[:, None, :], jnp.concatenate([excl, n8, base], axis=0)[None], 0), axis=-1)
    ex_b, n8_b, base_b = jnp.split(picked, 3, axis=1)
    p3 = pos[:, :, None]
    hit = (ex_b[:, None, :] <= p3) & (p3 < (ex_b + n8_b)[:, None, :])
    src = jnp.sum(jnp.where(hit, base_b[:, None, :] + p3, 0), axis=-1)
    trash = be[:, None] * cpb + j
    slots = jnp.where(valid, src, n_tiles * SORT_CHUNKS + trash).astype(I32)
    return be, nb_total.astype(I32), slots.reshape(-1)


def kernel(x, c, w_ada, b_ada, norm1_g, w_in, q_norm_g, k_norm_g, lambda_q1, lambda_k1, lambda_q2,
           lambda_k2, subln_g, conv_w, conv_b, conv_ln_g, conv_ln_b, w_out, norm2_g, w_router, b_router,
           w_gate_up, b_gate_up, w_down, b_down):
    B, S, D = x.shape
    N = B * S
    assert S % INPROJ_ROWS == 0 and S % ATTN_SUB_ROWS == 0 and S % TOK_TILE == 0
    n_tiles = N // TOK_TILE
    trash_tiles = _round_up(-(-(N_EXPERTS * CHUNKS_PER_BLOCK) // SORT_CHUNKS), ROUTE_TILES_PER_STEP)
    log2e = math.log2(math.e)
    slopes, qfeat, kfeat = _alibi_constants(S)
    xt = x.reshape(N, D)
    for l in range(w_ada.shape[0]):
        lam_init = 0.8 - 0.6 * math.exp(-0.3 * l)
        row = lambda a: a[l][None, :]
        mod, lam = _ada(c, w_ada[l], row(b_ada), row(lambda_q1), row(lambda_k1), row(lambda_q2),
                        row(lambda_k2), lam_init)
        mod3 = mod.reshape(B, N_MOD, D)
        qg = jnp.tile(q_norm_g[l], 2 * ATTN_HEADS)[None, :] * (ATTN_HEAD_DIM ** -0.5 * log2e)
        kg = jnp.tile(k_norm_g[l], 2 * ATTN_HEADS)[None, :]
        hglu, q, k, v = _inproj(xt, mod3, row(norm1_g), w_in[l].astype(BF16), qg, kg, S)
        attn = _attention(lam[0, 0:1], slopes, qfeat, kfeat, q, k, v, row(subln_g), B, S, lam_init)
        cw = jnp.pad(conv_w[l], ((0, 1), (0, 0)))
        x1 = _mixout(hglu, attn, xt, mod3, cw, row(conv_b), row(conv_ln_g), row(conv_ln_b),
                     w_out[l].astype(BF16), S)
        xs, rt, ce = _route(x1, mod3, row(norm2_g), w_router[l].T, b_router[l][:, None], S, trash_tiles)
        be, nb_total, slots = _block_tables(ce[:n_tiles], n_tiles)
        ys = _experts(be, nb_total, slots, xs, w_gate_up[l], b_gate_up[l][:, None, :],
                      w_down[l], b_down[l][:, None, :])
        rt_tok = rt[:n_tiles].transpose(0, 2, 1).reshape(N, 2 * TOP_K)
        xt = _combine(ys, rt_tok, x1, mod3, S)
    return xt.reshape(B, S, D)
```
